```python
import jax
import jax.numpy as jnp
from jax import lax
import numpy as np

D_MODEL = 2048
BATCH = 2
SEQ = 4096
DEPTH = 4
DEC_BATCH = 8
DEC_SEQ = 4
PAST_LEN = 16384
PAGE_SIZE = 128

N_BRANCH = 4
BRANCH_W = D_MODEL // 4
GMLP_CHUNK = 128
GMLP_GROUPS = 4
GMLP_GW = BRANCH_W // GMLP_GROUPS
LRU_BLOCKS = 4
LRU_BS = BRANCH_W // LRU_BLOCKS
LRU_C = 8.0
CONV_W = 4
SB_HEADS = 4
SB_HD = BRANCH_W // SB_HEADS
SB_QBLOCK = 128
SB_BIAS_INIT = -6.0
ML_HEADS = 4
ML_HD = BRANCH_W // ML_HEADS
ML_CHUNK = 128
EPS = 1e-6
IN_SPLITS = tuple(BRANCH_W * i for i in range(1, 13)) + (12 * BRANCH_W + ML_HEADS,)
N_IN = 12 * BRANCH_W + 2 * ML_HEADS

kernel_name = 'hybrid_gated_branch_decode_step'


def _lin_combine(c1, c2):
    a1, b1 = c1
    a2, b2 = c2
    return a1 * a2, a2 * b1 + b2


def _stack(lst):
    return jnp.stack(lst, axis=0)


def rms_norm(x, g):
    xf = x.astype(jnp.float32)
    return xf * lax.rsqrt(jnp.mean(xf * xf, axis=-1, keepdims=True) + EPS) * g.astype(jnp.float32)


def layer_norm(x, g, b):
    xf = x.astype(jnp.float32)
    xc = xf - jnp.mean(xf, axis=-1, keepdims=True)
    var = jnp.mean(xc * xc, axis=-1, keepdims=True)
    return xc * lax.rsqrt(var + EPS) * g + b


def block_diag(x, w):
    nb, bi, bo = w.shape
    y = jnp.einsum('...hi,hij->...hj', x.reshape(x.shape[:-1] + (nb, bi)), w)
    return y.reshape(x.shape[:-1] + (nb * bo,))


def causal_conv(x, hist, w, b):
    T = x.shape[1]
    xp = jnp.concatenate([hist.astype(x.dtype), x], axis=1)
    y = b + w[0] * xp[:, 0:T]
    for j in range(1, CONV_W):
        y = y + w[j] * xp[:, j:j + T]
    return y, xp[:, -(CONV_W - 1):]


def gmlp_branch(u, v, ln_g, ln_b, w_s, b_s):
    B, T, W = v.shape
    L = min(GMLP_CHUNK, T)
    vn = layer_norm(v, ln_g, ln_b)
    vg = vn.reshape(B, T // L, L, GMLP_GROUPS, GMLP_GW)
    wm = jnp.where(jnp.tril(jnp.ones((L, L), dtype=bool)), w_s[:, :L, :L].astype(jnp.float32), 0.0)
    s = jnp.einsum('gts,bcsge->bctge', wm, vg) + b_s[:, :L].T.astype(jnp.float32)[None, None, :, :, None]
    return u.astype(jnp.float32) * s.reshape(B, T, W), vn


def rglru_branch(x, conv_hist, h0, pos, conv_w, conv_b, w_a, b_a, w_x, b_x, lam):
    xc, hist = causal_conv(x, conv_hist, conv_w, conv_b)
    xf = xc.astype(jnp.float32)
    r = jax.nn.sigmoid(block_diag(xf, w_a) + b_a)
    i = jax.nn.sigmoid(block_diag(xf, w_x) + b_x)
    log_a = LRU_C * r * jax.nn.log_sigmoid(lam.astype(jnp.float32))
    a = jnp.exp(log_a)
    mult = jnp.where((pos == 0)[None, :, None], 1.0, jnp.sqrt(-jnp.expm1(2.0 * log_a)))
    u = mult * (i * xf)
    u = u.at[:, 0].add(a[:, 0] * h0.astype(jnp.float32))
    _, h = lax.associative_scan(_lin_combine, (a, u), axis=1)
    return h, hist, h[:, -1]


def sb_block(q, k, v, q_pos, k_pos, bias):
    z = jnp.einsum('bqhd,bkhd->bhqk', q, k) * (SB_HD ** -0.5) + bias[None, :, None, None]
    mask = k_pos[None, :] < q_pos[:, None]
    log_1mb = jnp.where(mask, jax.nn.log_sigmoid(-z), 0.0)
    suffix = lax.cumsum(log_1mb, axis=3, reverse=True) - log_1mb
    att = jnp.where(mask, jnp.exp(jax.nn.log_sigmoid(z) + suffix), 0.0)
    return jnp.einsum('bhqk,bkhd->bqhd', att, v)


def stick_breaking(q, k, v, q_pos, k_pos, bias):
    B, Tq, H, d = q.shape
    L = min(SB_QBLOCK, Tq)
    nb = Tq // L
    kf = k.astype(jnp.float32)
    vf = v.astype(jnp.float32)
    bf = bias.astype(jnp.float32)
    qb = jnp.moveaxis(q.astype(jnp.float32).reshape(B, nb, L, H, d), 1, 0)
    pb = q_pos.reshape(nb, L)
    out = lax.map(lambda a: sb_block(a[0], kf, vf, a[1], k_pos, bf), (qb, pb))
    return jnp.moveaxis(out, 0, 1).reshape(B, Tq, H * d)


def mlstm_chunk(carry, xs):
    C, n, m = carry
    q, k, v, logf, ig = xs
    L = q.shape[2]
    b = jnp.cumsum(logf, axis=-1)
    causal = jnp.tril(jnp.ones((L, L), dtype=bool))
    dmat = jnp.where(causal, b[..., :, None] - b[..., None, :] + ig[..., None, :], -jnp.inf)
    inter = b + m[..., None]
    m_t = jnp.maximum(jnp.max(dmat, axis=-1), inter)
    s = jnp.einsum('bhtd,bhsd->bhts', q, k) * jnp.exp(dmat - m_t[..., None])
    w_inter = jnp.exp(inter - m_t)
    num = jnp.einsum('bhts,bhsd->bhtd', s, v) + w_inter[..., None] * jnp.einsum('bhtd,bhde->bhte', q, C)
    den = jnp.sum(s, axis=-1) + w_inter * jnp.einsum('bhtd,bhd->bht', q, n)
    h = num / jnp.maximum(jnp.abs(den), jnp.exp(-m_t))[..., None]
    b_last = b[..., -1]
    g = b_last[..., None] - b + ig
    m_new = jnp.maximum(b_last + m, jnp.max(g, axis=-1))
    wk = jnp.exp(g - m_new[..., None])
    decay = jnp.exp(b_last + m - m_new)
    C_new = decay[..., None, None] * C + jnp.einsum('bhs,bhsd,bhse->bhde', wk, k, v)
    n_new = decay[..., None] * n + jnp.einsum('bhs,bhsd->bhd', wk, k)
    return (C_new, n_new, m_new), h


def mlstm_branch(x, o_pre, i_pre, f_pre, conv_hist, C0, n0, m0, conv_w, conv_b, w_q, w_k, w_v, b_i, b_f, norm_g, skip):
    B, T, W = x.shape
    xc, hist = causal_conv(x, conv_hist, conv_w, conv_b)
    xc = jax.nn.silu(xc.astype(jnp.float32))
    q = block_diag(xc, w_q)
    k = block_diag(xc, w_k) * (ML_HD ** -0.5)
    v = block_diag(x.astype(jnp.float32), w_v)
    L = min(ML_CHUNK, T)
    nC = T // L

    def heads(t):
        return t.reshape(B, T, ML_HEADS, ML_HD).transpose(0, 2, 1, 3)

    def chunks(t):
        return jnp.moveaxis(t.reshape(t.shape[:2] + (nC, L) + t.shape[3:]), 2, 0)

    logf = jax.nn.log_sigmoid(f_pre.astype(jnp.float32) + b_f).transpose(0, 2, 1)
    ig = (i_pre.astype(jnp.float32) + b_i).transpose(0, 2, 1)
    carry0 = (C0.astype(jnp.float32), n0.astype(jnp.float32), m0.astype(jnp.float32))
    (C, n, m), h = lax.scan(mlstm_chunk, carry0, (chunks(heads(q)), chunks(heads(k)), chunks(heads(v)), chunks(logf), chunks(ig)))
    h = jnp.moveaxis(h, 0, 2).reshape(B, ML_HEADS, T, ML_HD).transpose(0, 2, 1, 3)
    h = jax.nn.sigmoid(o_pre.astype(jnp.float32)).reshape(B, T, ML_HEADS, ML_HD) * h
    hc = h - jnp.mean(h, axis=-1, keepdims=True)
    hn = (hc * lax.rsqrt(jnp.mean(hc * hc, axis=-1, keepdims=True) + EPS)).reshape(B, T, W) * norm_g
    return hn + skip * xc, hist, C, n, m


def mixer(xn, P, pos, st, k_past, v_past):
    B, T, _ = xn.shape
    proj = xn @ P['w_in']
    a_u, a_v, a_z, b_x, b_z, c_q, c_k, c_v, c_z, d_x, d_z, d_o, d_i, d_f = jnp.split(proj, IN_SPLITS, axis=-1)
    y_a, gmlp_v = gmlp_branch(a_u, a_v, P['gmlp_ln_g'], P['gmlp_ln_b'], P['gmlp_ws'], P['gmlp_bs'])
    y_b, conv_b, h_b = rglru_branch(b_x, st['conv_b'], st['h_b'], pos, P['lru_conv_w'], P['lru_conv_b'],
                                    P['lru_wa'], P['lru_ba'], P['lru_wx'], P['lru_bx'], P['lru_lambda'])
    q = c_q.reshape(B, T, SB_HEADS, SB_HD)
    k = c_k.reshape(B, T, SB_HEADS, SB_HD)
    v = c_v.reshape(B, T, SB_HEADS, SB_HD)
    k_all = jnp.concatenate([k_past.astype(k.dtype), k], axis=1)
    v_all = jnp.concatenate([v_past.astype(v.dtype), v], axis=1)
    k_pos = jnp.arange(k_all.shape[1], dtype=jnp.int32)
    y_c = stick_breaking(q, k_all, v_all, pos, k_pos, P['sb_bias'])
    y_d, conv_d, C, n, m = mlstm_branch(d_x, d_o, d_i, d_f, st['conv_d'], st['C'], st['n'], st['m'],
                                        P['ml_conv_w'], P['ml_conv_b'], P['ml_wq'], P['ml_wk'], P['ml_wv'],
                                        P['ml_bi'], P['ml_bf'], P['ml_norm_g'], P['ml_skip'])
    ys = jnp.stack([y_a, y_b, y_c, y_d], axis=2)
    zs = jnp.stack([a_z, b_z, c_z, d_z], axis=2).astype(jnp.float32)
    branch = (ys * jax.nn.silu(zs)).astype(xn.dtype)
    p = jnp.einsum('btmw,mwd->btmd', branch, P['w_branch']).astype(jnp.float32)
    g = jax.nn.sigmoid((xn @ P['w_gate'] + P['b_gate']).astype(jnp.float32)).reshape(B, T, N_BRANCH, D_MODEL)
    merged = jnp.einsum('btmd,btmd->btd', g, p).astype(xn.dtype)
    out = merged @ P['w_out']
    new_st = dict(conv_b=conv_b, h_b=h_b, conv_d=conv_d, C=C, n=n, m=m)
    return out, new_st, k, v, gmlp_v


def setup_inputs(seed: int = 0) -> dict:
    key = jax.random.key(seed)
    ks = iter(jax.random.split(key, 64))
    f32 = jnp.float32

    def nrm(shape, scale):
        return jax.random.normal(next(ks), shape, f32) * scale

    W = BRANCH_W
    n_pages = PAST_LEN // PAGE_SIZE
    n_pool = (DEC_BATCH * n_pages * 5) // 4
    x_prompt = nrm((BATCH, SEQ, D_MODEL), 1.0)
    x_sample = nrm((DEC_BATCH, DEC_SEQ, D_MODEL), 1.0)
    cache_k = nrm((DEPTH, n_pool, PAGE_SIZE, SB_HEADS, SB_HD), 1.0)
    cache_v = nrm((DEPTH, n_pool, PAGE_SIZE, SB_HEADS, SB_HD), 1.0)
    page_table = jax.random.permutation(next(ks), n_pool)[: DEC_BATCH * n_pages].reshape(DEC_BATCH, n_pages).astype(jnp.int32)
    state_rglru_conv = nrm((DEPTH, DEC_BATCH, CONV_W - 1, W), 1.0)
    state_rglru_h = nrm((DEPTH, DEC_BATCH, W), 0.5)
    state_mlstm_conv = nrm((DEPTH, DEC_BATCH, CONV_W - 1, W), 1.0)
    state_mlstm_c = nrm((DEPTH, DEC_BATCH, ML_HEADS, ML_HD, ML_HD), 0.1)
    state_mlstm_n = nrm((DEPTH, DEC_BATCH, ML_HEADS, ML_HD), 0.1)
    state_mlstm_m = nrm((DEPTH, DEC_BATCH, ML_HEADS), 0.5)
    norm_pre = 1.0 + nrm((DEPTH, D_MODEL), 0.02)
    norm_post = 1.0 + nrm((DEPTH, D_MODEL), 0.02)
    w_in = nrm((DEPTH, D_MODEL, N_IN), D_MODEL ** -0.5)
    gmlp_ln_g = 1.0 + nrm((DEPTH, W), 0.02)
    gmlp_ln_b = nrm((DEPTH, W), 0.01)
    gmlp_ws = nrm((DEPTH, GMLP_GROUPS, GMLP_CHUNK, GMLP_CHUNK), GMLP_CHUNK ** -0.5)
    gmlp_bs = 1.0 + nrm((DEPTH, GMLP_GROUPS, GMLP_CHUNK), 0.1)
    lru_conv_w = nrm((DEPTH, CONV_W, W), CONV_W ** -0.5)
    lru_conv_b = nrm((DEPTH, W), 0.01)
    lru_wa = nrm((DEPTH, LRU_BLOCKS, LRU_BS, LRU_BS), LRU_BS ** -0.5)
    lru_ba = nrm((DEPTH, W), 0.01)
    lru_wx = nrm((DEPTH, LRU_BLOCKS, LRU_BS, LRU_BS), LRU_BS ** -0.5)
    lru_bx = nrm((DEPTH, W), 0.01)
    a8 = jax.random.uniform(next(ks), (DEPTH, W), f32, 0.9, 0.999)
    sa = a8 ** (1.0 / LRU_C)
    lru_lambda = jnp.log(sa) - jnp.log1p(-sa)
    ml_conv_w = nrm((DEPTH, CONV_W, W), CONV_W ** -0.5)
    ml_conv_b = nrm((DEPTH, W), 0.01)
    ml_wq = nrm((DEPTH, ML_HEADS, ML_HD, ML_HD), ML_HD ** -0.5)
    ml_wk = nrm((DEPTH, ML_HEADS, ML_HD, ML_HD), ML_HD ** -0.5)
    ml_wv = nrm((DEPTH, ML_HEADS, ML_HD, ML_HD), ML_HD ** -0.5)
    ml_bi = nrm((DEPTH, ML_HEADS), 0.1)
    ml_bf = jnp.linspace(3.0, 6.0, ML_HEADS, dtype=f32)[None, :] + nrm((DEPTH, ML_HEADS), 0.01)
    ml_norm_g = 1.0 + nrm((DEPTH, W), 0.02)
    ml_skip = 1.0 + nrm((DEPTH, W), 0.02)
    sb_bias = SB_BIAS_INIT + nrm((DEPTH, SB_HEADS), 0.1)
    w_branch = nrm((DEPTH, N_BRANCH, W, D_MODEL), W ** -0.5)
    w_gate = nrm((DEPTH, D_MODEL, N_BRANCH * D_MODEL), D_MODEL ** -0.5)
    b_gate = nrm((DEPTH, N_BRANCH * D_MODEL), 0.01)
    w_out = nrm((DEPTH, D_MODEL, D_MODEL), D_MODEL ** -0.5)
    return {'x_prompt': x_prompt, 'x_sample': x_sample, 'cache_k': cache_k, 'cache_v': cache_v,
            'page_table': page_table, 'state_rglru_conv': state_rglru_conv, 'state_rglru_h': state_rglru_h,
            'state_mlstm_conv': state_mlstm_conv, 'state_mlstm_c': state_mlstm_c, 'state_mlstm_n': state_mlstm_n,
            'state_mlstm_m': state_mlstm_m, 'norm_pre': norm_pre, 'norm_post': norm_post, 'w_in': w_in,
            'gmlp_ln_g': gmlp_ln_g, 'gmlp_ln_b': gmlp_ln_b, 'gmlp_ws': gmlp_ws, 'gmlp_bs': gmlp_bs,
            'lru_conv_w': lru_conv_w, 'lru_conv_b': lru_conv_b, 'lru_wa': lru_wa, 'lru_ba': lru_ba,
            'lru_wx': lru_wx, 'lru_bx': lru_bx, 'lru_lambda': lru_lambda, 'ml_conv_w': ml_conv_w,
            'ml_conv_b': ml_conv_b, 'ml_wq': ml_wq, 'ml_wk': ml_wk, 'ml_wv': ml_wv, 'ml_bi': ml_bi,
            'ml_bf': ml_bf, 'ml_norm_g': ml_norm_g, 'ml_skip': ml_skip, 'sb_bias': sb_bias,
            'w_branch': w_branch, 'w_gate': w_gate, 'b_gate': b_gate, 'w_out': w_out}


def reference(x_prompt, x_sample, cache_k, cache_v, page_table, state_rglru_conv, state_rglru_h,
              state_mlstm_conv, state_mlstm_c, state_mlstm_n, state_mlstm_m, norm_pre, norm_post, w_in,
              gmlp_ln_g, gmlp_ln_b, gmlp_ws, gmlp_bs, lru_conv_w, lru_conv_b, lru_wa, lru_ba, lru_wx, lru_bx,
              lru_lambda, ml_conv_w, ml_conv_b, ml_wq, ml_wk, ml_wv, ml_bi, ml_bf, ml_norm_g, ml_skip,
              sb_bias, w_branch, w_gate, b_gate, w_out):
    f32 = jnp.float32
    Bp, Tp, _ = x_prompt.shape
    Bs, Ts, _ = x_sample.shape
    past_len = page_table.shape[1] * cache_k.shape[2]
    pos_p = jnp.arange(Tp, dtype=jnp.int32)
    pos_s = past_len + jnp.arange(Ts, dtype=jnp.int32)
    st_p = dict(conv_b=jnp.zeros((Bp, CONV_W - 1, BRANCH_W), x_prompt.dtype), h_b=jnp.zeros((Bp, BRANCH_W), f32),
                conv_d=jnp.zeros((Bp, CONV_W - 1, BRANCH_W), x_prompt.dtype),
                C=jnp.zeros((Bp, ML_HEADS, ML_HD, ML_HD), f32), n=jnp.zeros((Bp, ML_HEADS, ML_HD), f32),
                m=jnp.zeros((Bp, ML_HEADS), f32))
    kv_empty = jnp.zeros((Bp, 0, SB_HEADS, SB_HD), x_prompt.dtype)
    res_p = dict(conv_b=[], h_b=[], conv_d=[], C=[], n=[], m=[])
    res_s = dict(conv_b=[], h_b=[], conv_d=[], C=[], n=[], m=[])
    kp_l, vp_l, ks_l, vs_l, gv_l = [], [], [], [], []
    xp, xs = x_prompt, x_sample
    for l in range(DEPTH):
        P = dict(w_in=w_in[l], gmlp_ln_g=gmlp_ln_g[l], gmlp_ln_b=gmlp_ln_b[l], gmlp_ws=gmlp_ws[l], gmlp_bs=gmlp_bs[l],
                 lru_conv_w=lru_conv_w[l], lru_conv_b=lru_conv_b[l], lru_wa=lru_wa[l], lru_ba=lru_ba[l],
                 lru_wx=lru_wx[l], lru_bx=lru_bx[l], lru_lambda=lru_lambda[l], ml_conv_w=ml_conv_w[l],
                 ml_conv_b=ml_conv_b[l], ml_wq=ml_wq[l], ml_wk=ml_wk[l], ml_wv=ml_wv[l], ml_bi=ml_bi[l],
                 ml_bf=ml_bf[l], ml_norm_g=ml_norm_g[l], ml_skip=ml_skip[l], sb_bias=sb_bias[l],
                 w_branch=w_branch[l], w_gate=w_gate[l], b_gate=b_gate[l], w_out=w_out[l])
        out, nst, k, v, _ = mixer(rms_norm(xp, norm_pre[l]).astype(xp.dtype), P, pos_p, st_p, kv_empty, kv_empty)
        xp = xp + rms_norm(out, norm_post[l]).astype(xp.dtype)
        for name in nst:
            res_p[name].append(nst[name])
        kp_l.append(k)
        vp_l.append(v)
        st_s = dict(conv_b=state_rglru_conv[l], h_b=state_rglru_h[l], conv_d=state_mlstm_conv[l],
                    C=state_mlstm_c[l], n=state_mlstm_n[l], m=state_mlstm_m[l])
        k_past = cache_k[l][page_table].reshape(Bs, past_len, SB_HEADS, SB_HD)
        v_past = cache_v[l][page_table].reshape(Bs, past_len, SB_HEADS, SB_HD)
        out, nst, k, v, gv = mixer(rms_norm(xs, norm_pre[l]).astype(xs.dtype), P, pos_s, st_s, k_past, v_past)
        xs = xs + rms_norm(out, norm_post[l]).astype(xs.dtype)
        for name in nst:
            res_s[name].append(nst[name])
        ks_l.append(k)
        vs_l.append(v)
        gv_l.append(gv)
    return (xp, xs, _stack(kp_l), _stack(vp_l), _stack(ks_l), _stack(vs_l),
            _stack(res_p['conv_b']), _stack(res_p['h_b']), _stack(res_s['conv_b']), _stack(res_s['h_b']),
            _stack(res_p['conv_d']), _stack(res_p['C']), _stack(res_p['n']), _stack(res_p['m']),
            _stack(res_s['conv_d']), _stack(res_s['C']), _stack(res_s['n']), _stack(res_s['m']),
            _stack(gv_l))
```

```python
import functools

import jax
import jax.numpy as jnp
from jax import lax
from jax.experimental import pallas as pl
from jax.experimental.pallas import tpu as pltpu

F32 = jnp.float32
BF16 = jnp.bfloat16
SDS = jax.ShapeDtypeStruct

EPS = 1e-6
N_BRANCH = 4
HEADS = 4
CONV_W = 4
CHUNK = 128
LRU_C = 8.0
LANE = 128
SUBLANE = 8
CONV_PAD = SUBLANE
DEC_ROWS = 16
VMEM_LIMIT = 56 * 1024 * 1024


def _log_sigmoid_pair(z):
    t = jnp.log1p(jnp.exp(-jnp.abs(z)))
    return jnp.minimum(z, 0.0) - t, -jnp.maximum(z, 0.0) - t


def _log_sigmoid(z):
    return _log_sigmoid_pair(z)[0]


def _silu(x):
    return x * jax.nn.sigmoid(x)


def _dot(a, b):
    return jnp.dot(a, b, preferred_element_type=F32)


def _dot_nt(a, b):
    return lax.dot_general(a, b, (((1,), (1,)), ((), ())), preferred_element_type=F32)


def _split(x):
    hi = x.astype(BF16)
    return hi, (x - hi.astype(F32)).astype(BF16)


def _cumsum_dot_right(x, m):
    hi, lo = _split(x)
    return _dot(hi, m) + _dot(lo, m)


def _cumsum_dot_left(m, x):
    hi, lo = _split(x)
    return _dot(m, hi) + _dot(m, lo)


def _params(*sem):
    return pltpu.CompilerParams(dimension_semantics=sem, vmem_limit_bytes=VMEM_LIMIT)


def _in_proj_kernel(x_ref, g_ref, w_ref, wif_ref, proj_ref, pif_ref, xn_ref):
    @pl.when(pl.program_id(1) == 0)
    def _():
        x = x_ref[...]
        ms = jnp.mean(x * x, axis=-1, keepdims=True)
        xn = (x * lax.rsqrt(ms + EPS) * g_ref[...]).astype(BF16)
        xn_ref[...] = xn
        pif_ref[...] = _dot(xn, wif_ref[...])

    proj_ref[...] = _dot(xn_ref[...], w_ref[...])


def _in_proj(x, g, w_main, w_if, tm, tn):
    rows, d = x.shape
    n = w_main.shape[1]
    return pl.pallas_call(
        _in_proj_kernel,
        grid=(rows // tm, n // tn),
        in_specs=[pl.BlockSpec((tm, d), lambda i, j: (i, 0)),
                  pl.BlockSpec((1, d), lambda i, j: (0, 0)),
                  pl.BlockSpec((d, tn), lambda i, j: (0, j)),
                  pl.BlockSpec((d, LANE), lambda i, j: (0, 0))],
        out_specs=[pl.BlockSpec((tm, tn), lambda i, j: (i, j)),
                   pl.BlockSpec((tm, LANE), lambda i, j: (i, 0)),
                   pl.BlockSpec((tm, d), lambda i, j: (i, 0))],
        out_shape=[SDS((rows, n), F32), SDS((rows, LANE), F32), SDS((rows, d), BF16)],
        compiler_params=_params("parallel", "arbitrary"),
        name="in_proj",
    )(x, g, w_main, w_if)


def _gmlp_kernel(u_ref, v_ref, z_ref, lg_ref, lb_ref, ws_ref, bst_ref, y_ref, *vn_out, n_chunks):
    v = v_ref[...]
    vc = v - jnp.mean(v, axis=-1, keepdims=True)
    var = jnp.mean(vc * vc, axis=-1, keepdims=True)
    vn = vc * lax.rsqrt(var + EPS) * lg_ref[...] + lb_ref[...]
    if vn_out:
        vn_out[0][...] = vn
    vnb = vn.astype(BF16)
    row = lax.broadcasted_iota(jnp.int32, (CHUNK, CHUNK), 0)
    col = lax.broadcasted_iota(jnp.int32, (CHUNK, CHUNK), 1)
    for g in range(HEADS):
        gs = slice(g * LANE, (g + 1) * LANE)
        wm = jnp.where(row >= col, ws_ref[g], 0.0).astype(BF16)
        bcol = bst_ref[:, g:g + 1]
        for c in range(n_chunks):
            ts = slice(c * CHUNK, (c + 1) * CHUNK)
            s = _dot(wm, vnb[ts, gs]) + bcol
            y_ref[ts, gs] = (u_ref[ts, gs] * s * _silu(z_ref[ts, gs])).astype(BF16)


def _gmlp(proj3, ln_g, ln_b, ws, bs_t, tt, emit_vn):
    b, t, _ = proj3.shape
    w = ln_g.shape[-1]
    col = lambda c: pl.BlockSpec((None, tt, w), lambda i, j: (i, j, c))
    full = lambda a: pl.BlockSpec(a.shape, lambda i, j: (0,) * a.ndim)
    out_specs = [pl.BlockSpec((None, tt, w), lambda i, j: (i, j, 0))]
    out_shape = [SDS((b, t, w), BF16)]
    if emit_vn:
        out_specs.append(pl.BlockSpec((None, tt, w), lambda i, j: (i, j, 0)))
        out_shape.append(SDS((b, t, w), F32))
    return pl.pallas_call(
        functools.partial(_gmlp_kernel, n_chunks=tt // CHUNK),
        grid=(b, t // tt),
        in_specs=[col(0), col(1), col(2), full(ln_g), full(ln_b), full(ws), full(bs_t)],
        out_specs=out_specs,
        out_shape=out_shape,
        compiler_params=_params("parallel", "parallel"),
        name="gmlp",
    )(proj3, proj3, proj3, ln_g, ln_b, ws, bs_t)


def _causal_conv(buf_ref, x, cw_ref, cb_ref, tt):
    h0 = CONV_PAD - (CONV_W - 1)
    buf_ref[CONV_PAD:CONV_PAD + tt, :] = x
    y = cb_ref[...] + cw_ref[0:1, :] * buf_ref[h0:h0 + tt, :]
    for j in range(1, CONV_W):
        y = y + cw_ref[j:j + 1, :] * buf_ref[h0 + j:h0 + j + tt, :]
    buf_ref[h0:CONV_PAD, :] = x[tt - (CONV_W - 1):tt, :]
    return y


def _rglru_kernel(x_ref, z_ref, hist_ref, h0_ref, cw_ref, cb_ref, wa_ref, ba_ref, wx_ref, bx_ref, lam_ref,
                  y_ref, hist_out_ref, hlast_ref, buf_ref, hcar_ref, *, tt, t_valid, first_pos_zero):
    t = pl.program_id(1)
    h0 = CONV_PAD - (CONV_W - 1)

    @pl.when(t == 0)
    def _():
        buf_ref[h0:CONV_PAD, :] = hist_ref[...]
        hcar_ref[...] = h0_ref[...]

    x = x_ref[...]
    xc = _causal_conv(buf_ref, x, cw_ref, cb_ref, tt)
    xcb = xc.astype(BF16)
    ra, rx = [], []
    for blk in range(HEADS):
        bs = slice(blk * LANE, (blk + 1) * LANE)
        ra.append(_dot(xcb[:, bs], wa_ref[blk]))
        rx.append(_dot(xcb[:, bs], wx_ref[blk]))
    r = jax.nn.sigmoid(jnp.concatenate(ra, axis=-1) + ba_ref[...])
    i = jax.nn.sigmoid(jnp.concatenate(rx, axis=-1) + bx_ref[...])
    log_a = LRU_C * r * _log_sigmoid(lam_ref[...])
    a = jnp.exp(log_a)
    mult = jnp.sqrt(-jnp.tanh(log_a) * (a * a + 1.0))
    rowi = lax.broadcasted_iota(jnp.int32, (tt, 1), 0)
    if first_pos_zero:
        mult = jnp.where(rowi + t * tt == 0, 1.0, mult)
    u = mult * (i * xc)
    d = 1
    while d < tt:
        keep = rowi >= d
        a_sh = jnp.where(keep, pltpu.roll(a, d, axis=0), 1.0)
        u_sh = jnp.where(keep, pltpu.roll(u, d, axis=0), 0.0)
        u = a * u_sh + u
        a = a * a_sh
        d *= 2
    h = a * hcar_ref[...] + u
    hcar_ref[...] = h[tt - 1:tt, :]
    y_ref[...] = (h * _silu(z_ref[...])).astype(BF16)

    @pl.when(t == pl.num_programs(1) - 1)
    def _():
        hist_out_ref[...] = x[t_valid - (CONV_W - 1):t_valid, :]
        hlast_ref[...] = h[t_valid - 1:t_valid, :]


def _rglru(proj3, hist, h0, cw, cb, wa, ba, wx, bx, lam, tt, t_valid, first_pos_zero):
    b, t, _ = proj3.shape
    w = cb.shape[-1]
    assert t_valid >= CONV_W - 1 and (t_valid == tt or t == tt)
    col = lambda c: pl.BlockSpec((None, tt, w), lambda i, j: (i, j, c))
    full = lambda a: pl.BlockSpec(a.shape, lambda i, j: (0,) * a.ndim)
    per_b = lambda r: pl.BlockSpec((None, r, w), lambda i, j: (i, 0, 0))
    return pl.pallas_call(
        functools.partial(_rglru_kernel, tt=tt, t_valid=t_valid, first_pos_zero=first_pos_zero),
        grid=(b, t // tt),
        in_specs=[col(3), col(4), per_b(CONV_W - 1), per_b(1)]
                 + [full(a) for a in (cw, cb, wa, ba, wx, bx, lam)],
        out_specs=[pl.BlockSpec((None, tt, w), lambda i, j: (i, j, 0)), per_b(CONV_W - 1), per_b(1)],
        out_shape=[SDS((b, t, w), BF16), SDS((b, CONV_W - 1, w), F32), SDS((b, 1, w), F32)],
        scratch_shapes=[pltpu.VMEM((CONV_PAD + tt, w), F32), pltpu.VMEM((1, w), F32)],
        compiler_params=_params("parallel", "arbitrary"),
        name="rglru",
    )(proj3, proj3, hist, h0, cw, cb, wa, ba, wx, bx, lam)


def _sb_block(qb, kblk, vblk, r, later, bias, scale, mask):
    z = _dot_nt(qb, kblk) * scale + bias
    log_b, log_1mb = _log_sigmoid_pair(z)
    if mask is not None:
        log_1mb = jnp.where(mask, log_1mb, 0.0)
    att = jnp.exp(log_b + _cumsum_dot_right(log_1mb, later) + r)
    if mask is not None:
        att = jnp.where(mask, att, 0.0)
    return _dot(att.astype(BF16), vblk), jnp.sum(log_1mb, axis=-1, keepdims=True)


def _later_matrix():
    row = lax.broadcasted_iota(jnp.int32, (CHUNK, CHUNK), 0)
    col = lax.broadcasted_iota(jnp.int32, (CHUNK, CHUNK), 1)
    return (row > col).astype(BF16)


def _sb_attn_kernel(bias_ref, q_ref, k_ref, v_ref, z_ref, y_ref, kb_ref, vb_ref, acc_ref, r_ref, *, tq, scale):
    qi = pl.program_id(2)
    tk = CHUNK

    @pl.when(qi == 0)
    def _():
        kb_ref[...] = k_ref[...].astype(BF16)
        vb_ref[...] = v_ref[...].astype(BF16)

    bias = bias_ref[pl.program_id(1)]
    later = _later_matrix()
    acc_ref[...] = jnp.zeros_like(acc_ref)
    r_ref[...] = jnp.zeros_like(r_ref)
    nsub = tq // tk
    for sub in reversed(range(nsub)):
        r0 = sub * tk
        k0 = pl.multiple_of(qi * tq + r0, tk)
        rr = lax.broadcasted_iota(jnp.int32, (tq - r0, tk), 0)
        cc = lax.broadcasted_iota(jnp.int32, (tq - r0, tk), 1)
        pv, ds = _sb_block(q_ref[r0:tq, :].astype(BF16), kb_ref[pl.ds(k0, tk), :], vb_ref[pl.ds(k0, tk), :],
                           r_ref[r0:tq, :], later, bias, scale, cc < rr)
        acc_ref[r0:tq, :] += pv
        r_ref[r0:tq, :] += ds

    qb = q_ref[...].astype(BF16)
    nprev = qi * nsub

    def body(i, carry):
        k0 = pl.multiple_of((nprev - 1 - i) * tk, tk)
        pv, ds = _sb_block(qb, kb_ref[pl.ds(k0, tk), :], vb_ref[pl.ds(k0, tk), :], r_ref[...],
                           later, bias, scale, None)
        acc_ref[...] += pv
        r_ref[...] += ds
        return carry

    lax.fori_loop(0, nprev, body, 0)
    y_ref[...] = (acc_ref[...] * _silu(z_ref[...])).astype(BF16)


def _sb_attn(proj3, bias, tq):
    b, t, _ = proj3.shape
    hd = LANE
    qcol, kcol, vcol, zcol = (5 * HEADS, 6 * HEADS, 7 * HEADS, 8 * HEADS)
    tile = lambda c0: pl.BlockSpec((None, tq, hd), lambda i, h, j: (i, j, c0 + h))
    whole = lambda c0: pl.BlockSpec((None, t, hd), lambda i, h, j: (i, 0, c0 + h))
    return pl.pallas_call(
        functools.partial(_sb_attn_kernel, tq=tq, scale=hd ** -0.5),
        grid=(b, HEADS, t // tq),
        in_specs=[pl.BlockSpec(memory_space=pltpu.SMEM), tile(qcol), whole(kcol), whole(vcol), tile(zcol)],
        out_specs=pl.BlockSpec((None, tq, hd), lambda i, h, j: (i, j, h)),
        out_shape=SDS((b, t, HEADS * hd), BF16),
        scratch_shapes=[pltpu.VMEM((t, hd), BF16), pltpu.VMEM((t, hd), BF16),
                        pltpu.VMEM((tq, hd), F32), pltpu.VMEM((tq, 1), F32)],
        compiler_params=_params("parallel", "parallel", "arbitrary"),
        name="sb_attn",
    )(bias, proj3, proj3, proj3, proj3)


def _sb_decode_kernel(pt_ref, bias_ref, q_ref, kn_ref, vn_ref, z_ref, *rest, n_pages, scale):
    k_pages, v_pages = rest[:n_pages], rest[n_pages:2 * n_pages]
    y_ref, acc_ref, r_ref = rest[2 * n_pages:]
    s = pl.program_id(1)
    later = _later_matrix()

    def update(h, kblk, vblk, mask):
        hs = slice(h * LANE, (h + 1) * LANE)
        pv, ds = _sb_block(q_ref[:, hs].astype(BF16), kblk.astype(BF16), vblk.astype(BF16), r_ref[h],
                           later, bias_ref[h], scale, mask)
        acc_ref[h] += pv
        r_ref[h] += jnp.broadcast_to(ds, r_ref.shape[1:])

    @pl.when(s == 0)
    def _():
        acc_ref[...] = jnp.zeros_like(acc_ref)
        r_ref[...] = jnp.zeros_like(r_ref)
        rr = lax.broadcasted_iota(jnp.int32, (DEC_ROWS, CHUNK), 0)
        cc = lax.broadcasted_iota(jnp.int32, (DEC_ROWS, CHUNK), 1)
        for h in range(HEADS):
            hs = slice(h * LANE, (h + 1) * LANE)
            update(h, kn_ref[:, hs], vn_ref[:, hs], cc < rr)

    for g in range(n_pages):
        for h in range(HEADS):
            update(h, k_pages[g][pl.ds(h, CHUNK, stride=HEADS), :], v_pages[g][pl.ds(h, CHUNK, stride=HEADS), :], None)

    @pl.when(s == pl.num_programs(1) - 1)
    def _():
        acc = jnp.concatenate([acc_ref[h] for h in range(HEADS)], axis=-1)
        y_ref[...] = (acc * _silu(z_ref[...])).astype(BF16)


def _sb_decode(page_table, bias, q, k_new, v_new, z, cache_k, cache_v, layer, pages_per_step):
    b, n_tab = page_table.shape
    w = q.shape[-1]
    g = pages_per_step
    assert n_tab % g == 0
    page_rows, hd = cache_k.shape[2:]
    per_b = lambda r: pl.BlockSpec((None, r, w), lambda i, s, pt, bs: (i, 0, 0))

    def page_spec(j):
        return pl.BlockSpec((None, None, page_rows, hd),
                            lambda i, s, pt, bs: (layer, pt[i, n_tab - 1 - (s * g + j)], 0, 0))

    grid_spec = pltpu.PrefetchScalarGridSpec(
        num_scalar_prefetch=2,
        grid=(b, n_tab // g),
        in_specs=[per_b(DEC_ROWS), per_b(CHUNK), per_b(CHUNK), per_b(DEC_ROWS)]
                 + [page_spec(j) for j in range(g)] * 2,
        out_specs=per_b(DEC_ROWS),
        scratch_shapes=[pltpu.VMEM((HEADS, DEC_ROWS, hd), F32), pltpu.VMEM((HEADS, DEC_ROWS, hd), F32)],
    )
    return pl.pallas_call(
        functools.partial(_sb_decode_kernel, n_pages=g, scale=hd ** -0.5),
        grid_spec=grid_spec,
        out_shape=SDS((b, DEC_ROWS, w), BF16),
        compiler_params=_params("parallel", "arbitrary"),
        name="sb_decode",
    )(page_table, bias, q, k_new, v_new, z, *([cache_k] * g), *([cache_v] * g))


def _mlstm_kernel(x_ref, z_ref, o_ref, gc_ref, gr_ref, hist_ref, c0_ref, n0_ref, m0_ref,
                  cw_ref, cb_ref, wq_ref, wk_ref, wv_ref, brow_ref, bcol_ref, ng_ref, sk_ref,
                  y_ref, hist_out_ref, c_out_ref, n_out_ref, m_out_ref,
                  buf_ref, c_scr, n_scr, m_scr, *, t_valid, scale):
    c = pl.program_id(1)
    L = CHUNK
    h0 = CONV_PAD - (CONV_W - 1)

    @pl.when(c == 0)
    def _():
        buf_ref[h0:CONV_PAD, :] = hist_ref[...]
        c_scr[...] = c0_ref[...]
        n_scr[...] = n0_ref[...]
        m_scr[...] = m0_ref[...]

    x = x_ref[...]
    xc = _silu(_causal_conv(buf_ref, x, cw_ref, cb_ref, L))
    xcb = xc.astype(BF16)
    xb = x.astype(BF16)
    gc = gc_ref[...] + brow_ref[...]
    gr = gr_ref[...] + bcol_ref[...]
    row = lax.broadcasted_iota(jnp.int32, (L, L), 0)
    col = lax.broadcasted_iota(jnp.int32, (L, L), 1)
    causal = row >= col
    bc_all = _cumsum_dot_left(causal.astype(BF16), _log_sigmoid(gc))
    br_all = _cumsum_dot_right(_log_sigmoid(gr), (row <= col).astype(BF16))
    rowi = lax.broadcasted_iota(jnp.int32, (L, 1), 0)
    outs = []
    for h in range(HEADS):
        hs = slice(h * LANE, (h + 1) * LANE)
        q = _dot(xcb[:, hs], wq_ref[h])
        k = _dot(xcb[:, hs], wk_ref[h]) * scale
        vb = _dot(xb[:, hs], wv_ref[h]).astype(BF16)
        qb = q.astype(BF16)
        ig_c, b_c = gc[:, h:h + 1], bc_all[:, HEADS + h:HEADS + h + 1]
        ig_r, b_r = gr[h:h + 1, :], br_all[HEADS + h:HEADS + h + 1, :]
        m_h = m_scr[h:h + 1, 0:1]
        c_h = c_scr[h]
        n_h = n_scr[h:h + 1, :]
        dmat = jnp.where(causal, b_c - b_r + ig_r, -jnp.inf)
        inter = b_c + m_h
        m_t = jnp.maximum(jnp.max(dmat, axis=-1, keepdims=True), inter)
        s = _dot_nt(qb, k.astype(BF16)) * jnp.exp(dmat - m_t)
        w_inter = jnp.exp(inter - m_t)
        num = _dot(s.astype(BF16), vb) + w_inter * _dot(qb, c_h.astype(BF16))
        den = jnp.sum(s, axis=-1, keepdims=True) + w_inter * jnp.sum(q * n_h, axis=-1, keepdims=True)
        hh = num / jnp.maximum(jnp.abs(den), jnp.exp(-m_t))
        b_last = b_c[t_valid - 1:t_valid, :]
        g = b_last - b_c + ig_c
        if t_valid < L:
            g = jnp.where(rowi < t_valid, g, -jnp.inf)
        m_new = jnp.maximum(b_last + m_h, jnp.max(g, axis=0, keepdims=True))
        kw = k * jnp.exp(g - m_new)
        decay = jnp.exp(b_last + m_h - m_new)
        c_scr[h] = decay * c_h + _dot(kw.T.astype(BF16), vb)
        n_scr[h:h + 1, :] = decay * n_h + jnp.sum(kw, axis=0, keepdims=True)
        m_scr[h:h + 1, :] = jnp.broadcast_to(m_new, (1, LANE))
        hh = jax.nn.sigmoid(o_ref[:, hs]) * hh
        hc = hh - jnp.mean(hh, axis=-1, keepdims=True)
        outs.append(hc * lax.rsqrt(jnp.mean(hc * hc, axis=-1, keepdims=True) + EPS))
    hn = jnp.concatenate(outs, axis=-1) * ng_ref[...] + sk_ref[...] * xc
    y_ref[...] = (hn * _silu(z_ref[...])).astype(BF16)

    @pl.when(c == pl.num_programs(1) - 1)
    def _():
        hist_out_ref[...] = x[t_valid - (CONV_W - 1):t_valid, :]
        c_out_ref[...] = c_scr[...]
        n_out_ref[...] = n_scr[...]
        m_out_ref[...] = m_scr[...]


def _mlstm(proj3, gates_col, gates_row, hist, c0, n0, m0, cw, cb, wq, wk, wv, b_row, b_col, ng, sk, t_valid):
    b, t, _ = proj3.shape
    w = cb.shape[-1]
    hd = w // HEADS
    L = CHUNK
    assert t_valid >= CONV_W - 1 and (t_valid == L or t == L)
    col = lambda c: pl.BlockSpec((None, L, w), lambda i, j: (i, j, c))
    full = lambda a: pl.BlockSpec(a.shape, lambda i, j: (0,) * a.ndim)
    per_b = lambda *s: pl.BlockSpec((None,) + s, lambda i, j: (i,) + (0,) * len(s))
    return pl.pallas_call(
        functools.partial(_mlstm_kernel, t_valid=t_valid, scale=hd ** -0.5),
        grid=(b, t // L),
        in_specs=[col(9), col(10), col(11),
                  pl.BlockSpec((None, L, LANE), lambda i, j: (i, j, 0)),
                  pl.BlockSpec((None, 2 * HEADS, L), lambda i, j: (i, 0, j)),
                  per_b(CONV_W - 1, w), per_b(HEADS, hd, hd), per_b(HEADS, hd), per_b(HEADS, LANE)]
                 + [full(a) for a in (cw, cb, wq, wk, wv, b_row, b_col, ng, sk)],
        out_specs=[pl.BlockSpec((None, L, w), lambda i, j: (i, j, 0)),
                   per_b(CONV_W - 1, w), per_b(HEADS, hd, hd), per_b(HEADS, hd), per_b(HEADS, LANE)],
        out_shape=[SDS((b, t, w), BF16), SDS((b, CONV_W - 1, w), F32), SDS((b, HEADS, hd, hd), F32),
                   SDS((b, HEADS, hd), F32), SDS((b, HEADS, LANE), F32)],
        scratch_shapes=[pltpu.VMEM((CONV_PAD + L, w), F32), pltpu.VMEM((HEADS, hd, hd), F32),
                        pltpu.VMEM((HEADS, hd), F32), pltpu.VMEM((HEADS, LANE), F32)],
        compiler_params=_params("parallel", "arbitrary"),
        name="mlstm",
    )(proj3, proj3, proj3, gates_col, gates_row, hist, c0, n0, m0, cw, cb, wq, wk, wv, b_row, b_col, ng, sk)


def _merge_kernel(xn_ref, a_ref, b_ref, c_ref, d_ref, wg0_ref, wg1_ref, wg2_ref, wg3_ref, bg_ref, wb_ref, o_ref):
    xn = xn_ref[...]
    acc = None
    branches = (a_ref, b_ref, c_ref, d_ref)
    gates = (wg0_ref, wg1_ref, wg2_ref, wg3_ref)
    for m in range(N_BRANCH):
        gate = jax.nn.sigmoid(_dot(xn, gates[m][...]) + bg_ref[m:m + 1, :])
        term = gate * _dot(branches[m][...], wb_ref[m])
        acc = term if acc is None else acc + term
    o_ref[...] = acc.astype(BF16)


def _merge(xn, branches, w_gate, b_gate, w_branch, tm, tn):
    rows, d = xn.shape
    w = branches[0].shape[-1]
    nj = d // tn
    gate_spec = lambda m: pl.BlockSpec((d, tn), lambda i, j: (0, m * nj + j))
    return pl.pallas_call(
        _merge_kernel,
        grid=(rows // tm, nj),
        in_specs=[pl.BlockSpec((tm, d), lambda i, j: (i, 0))]
                 + [pl.BlockSpec((tm, w), lambda i, j: (i, 0))] * N_BRANCH
                 + [gate_spec(m) for m in range(N_BRANCH)]
                 + [pl.BlockSpec((N_BRANCH, tn), lambda i, j: (0, j)),
                    pl.BlockSpec((N_BRANCH, w, tn), lambda i, j: (0, 0, j))],
        out_specs=pl.BlockSpec((tm, tn), lambda i, j: (i, j)),
        out_shape=SDS((rows, d), BF16),
        compiler_params=_params("parallel", "arbitrary"),
        name="merge",
    )(xn, *branches, w_gate, w_gate, w_gate, w_gate, b_gate, w_branch)


def _out_proj_kernel(m_ref, x_ref, w_ref, g_ref, o_ref):
    out = _dot(m_ref[...], w_ref[...])
    ms = jnp.mean(out * out, axis=-1, keepdims=True)
    o_ref[...] = x_ref[...] + out * lax.rsqrt(ms + EPS) * g_ref[...]


def _out_proj(merged, x, w_out, g, tm):
    rows, d = x.shape
    return pl.pallas_call(
        _out_proj_kernel,
        grid=(rows // tm,),
        in_specs=[pl.BlockSpec((tm, d), lambda i: (i, 0)),
                  pl.BlockSpec((tm, d), lambda i: (i, 0)),
                  pl.BlockSpec((d, d), lambda i: (0, 0)),
                  pl.BlockSpec((1, d), lambda i: (0, 0))],
        out_specs=pl.BlockSpec((tm, d), lambda i: (i, 0)),
        out_shape=SDS((rows, d), F32),
        compiler_params=_params("parallel"),
        name="out_proj",
    )(merged, x, w_out, g)


def _tile(n, pref):
    return pref if n % pref == 0 else n


def _branches(proj3, gates_col, P, st, t_valid, first_pos_zero, emit_vn):
    b, t, _ = proj3.shape
    gm = _gmlp(proj3, P['ln_g'], P['ln_b'], P['ws'], P['bs_t'], _tile(t, 4 * CHUNK), emit_vn)
    tt = _tile(t, 4 * CHUNK) if t_valid == CHUNK else t
    y_b, hist_b, h_b = _rglru(proj3, st['conv_b'], st['h_b'], P['lru_cw'], P['lru_cb'], P['lru_wa'], P['lru_ba'],
                              P['lru_wx'], P['lru_bx'], P['lru_lam'], tt, tt if t_valid == CHUNK else t_valid,
                              first_pos_zero)
    gates_row = jnp.swapaxes(gates_col[:, :, :2 * HEADS], 1, 2)
    y_d, hist_d, c, n, m = _mlstm(proj3, gates_col, gates_row, st['conv_d'], st['C'], st['n'], st['m'],
                                  P['ml_cw'], P['ml_cb'], P['ml_wq'], P['ml_wk'], P['ml_wv'], P['ml_brow'],
                                  P['ml_bcol'], P['ml_ng'], P['ml_sk'], t_valid)
    new_st = dict(conv_b=hist_b, h_b=h_b[:, 0], conv_d=hist_d, C=c, n=n, m=m[:, :, 0])
    return gm, y_b, y_d, new_st


def _dense_tail(x, xn, branches, P, tm):
    merged = _merge(xn, branches, P['w_gate'], P['b_gate'], P['w_branch'], tm, _tile(x.shape[1], 256))
    return _out_proj(merged, x, P['w_out'], P['norm_post'], _tile(x.shape[0], 256))


def _state_in(conv_b, h_b, conv_d, c, n, m):
    return dict(conv_b=conv_b, h_b=h_b[:, None, :], conv_d=conv_d, C=c, n=n,
                m=jnp.broadcast_to(m[..., None], m.shape + (LANE,)))


def kernel(x_prompt, x_sample, cache_k, cache_v, page_table, state_rglru_conv, state_rglru_h, state_mlstm_conv, state_mlstm_c, state_mlstm_n, state_mlstm_m, norm_pre, norm_post, w_in, gmlp_ln_g, gmlp_ln_b, gmlp_ws, gmlp_bs, lru_conv_w, lru_conv_b, lru_wa, lru_ba, lru_wx, lru_bx, lru_lambda, ml_conv_w, ml_conv_b, ml_wq, ml_wk, ml_wv, ml_bi, ml_bf, ml_norm_g, ml_skip, sb_bias, w_branch, w_gate, b_gate, w_out):
    bp, tp, d = x_prompt.shape
    bs, ts, _ = x_sample.shape
    depth = w_in.shape[0]
    w = d // N_BRANCH
    hd = w // HEADS
    n_main = 12 * w
    page = cache_k.shape[2]
    past_len = page_table.shape[1] * page
    assert tp % CHUNK == 0 and page == CHUNK and ts <= DEC_ROWS and cache_k.shape[3:] == (HEADS, hd)
    ck = cache_k.reshape(depth, cache_k.shape[1], page * HEADS, hd)
    cv = cache_v.reshape(depth, cache_v.shape[1], page * HEADS, hd)
    pages_per_step = 8 if page_table.shape[1] % 8 == 0 else 1

    xp = x_prompt.reshape(bp * tp, d)
    xs = x_sample.reshape(bs * ts, d)
    zeros_p = _state_in(jnp.zeros((bp, CONV_W - 1, w), F32), jnp.zeros((bp, w), F32),
                        jnp.zeros((bp, CONV_W - 1, w), F32), jnp.zeros((bp, HEADS, hd, hd), F32),
                        jnp.zeros((bp, HEADS, hd), F32), jnp.zeros((bp, HEADS), F32))
    names = ('conv_b', 'h_b', 'conv_d', 'C', 'n', 'm')
    res_p = {k: [] for k in names}
    res_s = {k: [] for k in names}
    kp_l, vp_l, ks_l, vs_l, gv_l = [], [], [], [], []
    row2 = lambda a: a[None, :]

    for l in range(depth):
        w_if = jnp.pad(w_in[l, :, n_main:], ((0, 0), (0, LANE - 2 * HEADS))).astype(BF16)
        gate_bias = jnp.concatenate([ml_bi[l], ml_bf[l]])
        P = dict(
            ln_g=row2(gmlp_ln_g[l]), ln_b=row2(gmlp_ln_b[l]), ws=gmlp_ws[l], bs_t=gmlp_bs[l].T,
            lru_cw=lru_conv_w[l], lru_cb=row2(lru_conv_b[l]), lru_wa=lru_wa[l].astype(BF16), lru_ba=row2(lru_ba[l]),
            lru_wx=lru_wx[l].astype(BF16), lru_bx=row2(lru_bx[l]), lru_lam=row2(lru_lambda[l]),
            ml_cw=ml_conv_w[l], ml_cb=row2(ml_conv_b[l]), ml_wq=ml_wq[l].astype(BF16), ml_wk=ml_wk[l].astype(BF16),
            ml_wv=ml_wv[l].astype(BF16), ml_brow=jnp.pad(gate_bias, (0, LANE - 2 * HEADS))[None, :],
            ml_bcol=gate_bias[:, None], ml_ng=row2(ml_norm_g[l]), ml_sk=row2(ml_skip[l]),
            w_gate=w_gate[l].astype(BF16), b_gate=b_gate[l].reshape(N_BRANCH, d),
            w_branch=w_branch[l].astype(BF16), w_out=w_out[l].astype(BF16), norm_post=row2(norm_post[l]))
        w_main = w_in[l, :, :n_main].astype(BF16)
        g_pre = row2(norm_pre[l])

        proj, pif, xn = _in_proj(xp, g_pre, w_main, w_if, _tile(bp * tp, 512), 512)
        proj3 = proj.reshape(bp, tp, n_main)
        y_a, y_b, y_d, nst = _branches(proj3, pif.reshape(bp, tp, LANE), P, zeros_p, CHUNK, True, False)
        y_c = _sb_attn(proj3, sb_bias[l], _tile(tp, 2 * CHUNK))
        xp = _dense_tail(xp, xn, [y.reshape(bp * tp, w) for y in (y_a[0], y_b, y_c, y_d)], P, _tile(bp * tp, 512))
        for k in names:
            res_p[k].append(nst[k])
        kp_l.append(proj3[:, :, 6 * w:7 * w].reshape(bp, tp, HEADS, hd))
        vp_l.append(proj3[:, :, 7 * w:8 * w].reshape(bp, tp, HEADS, hd))

        proj, pif, xn = _in_proj(xs, g_pre, w_main, w_if, bs * ts, 512)
        pad_t = lambda a, rows: jnp.pad(a, ((0, 0), (0, rows - ts), (0, 0)))
        proj3 = proj.reshape(bs, ts, n_main)
        projc = pad_t(proj3, CHUNK)
        st_s = _state_in(state_rglru_conv[l], state_rglru_h[l], state_mlstm_conv[l], state_mlstm_c[l],
                         state_mlstm_n[l], state_mlstm_m[l])
        (y_a, vn), y_b, y_d, nst = _branches(projc, pad_t(pif.reshape(bs, ts, LANE), CHUNK), P, st_s, ts,
                                             past_len == 0, True)
        y_c = _sb_decode(page_table, sb_bias[l], pad_t(proj3[:, :, 5 * w:6 * w], DEC_ROWS),
                         projc[:, :, 6 * w:7 * w], projc[:, :, 7 * w:8 * w], pad_t(proj3[:, :, 8 * w:9 * w], DEC_ROWS),
                         ck, cv, l, pages_per_step)
        xs = _dense_tail(xs, xn, [y[:, :ts].reshape(bs * ts, w) for y in (y_a, y_b, y_c, y_d)], P, bs * ts)
        for k in names:
            res_s[k].append(nst[k])
        ks_l.append(proj3[:, :, 6 * w:7 * w].reshape(bs, ts, HEADS, hd))
        vs_l.append(proj3[:, :, 7 * w:8 * w].reshape(bs, ts, HEADS, hd))
        gv_l.append(vn[:, :ts])

    st = lambda lst: jnp.stack(lst, axis=0)
    return (xp.reshape(bp, tp, d), xs.reshape(bs, ts, d), st(kp_l), st(vp_l), st(ks_l), st(vs_l),
            st(res_p['conv_b']), st(res_p['h_b']), st(res_s['conv_b']), st(res_s['h_b']),
            st(res_p['conv_d']), st(res_p['C']), st(res_p['n']), st(res_p['m']),
            st(res_s['conv_d']), st(res_s['C']), st(res_s['n']), st(res_s['m']),
            st(gv_l))
```

```python
import functools

import jax
import jax.numpy as jnp
from jax import lax
from jax.experimental import pallas as pl
from jax.experimental.pallas import tpu as pltpu

F32 = jnp.float32
BF16 = jnp.bfloat16
SDS = jax.ShapeDtypeStruct

EPS = 1e-6
N_BRANCH = 4
HEADS = 4
CONV_W = 4
CHUNK = 128
LRU_C = 8.0
LANE = 128
SUBLANE = 8
CONV_PAD = SUBLANE
DEC_ROWS = 16
VMEM_LIMIT = 56 * 1024 * 1024


def _log_sigmoid_pair(z):
    t = jnp.log1p(jnp.exp(-jnp.abs(z)))
    return jnp.minimum(z, 0.0) - t, -jnp.maximum(z, 0.0) - t


def _log_sigmoid(z):
    return _log_sigmoid_pair(z)[0]


def _silu(x):
    return x * jax.nn.sigmoid(x)


def _dot(a, b):
    return jnp.dot(a, b, preferred_element_type=F32)


def _dot_nt(a, b):
    return lax.dot_general(a, b, (((1,), (1,)), ((), ())), preferred_element_type=F32)


def _split(x):
    hi = x.astype(BF16)
    return hi, (x - hi.astype(F32)).astype(BF16)


def _cumsum_dot_right(x, m):
    hi, lo = _split(x)
    return _dot(hi, m) + _dot(lo, m)


def _cumsum_dot_left(m, x):
    hi, lo = _split(x)
    return _dot(m, hi) + _dot(m, lo)


def _params(*sem):
    return pltpu.CompilerParams(dimension_semantics=sem, vmem_limit_bytes=VMEM_LIMIT)


def _in_proj_kernel(x_ref, g_ref, w_ref, wif_ref, proj_ref, pif_ref, xn_ref):
    @pl.when(pl.program_id(1) == 0)
    def _():
        x = x_ref[...]
        ms = jnp.mean(x * x, axis=-1, keepdims=True)
        xn = (x * lax.rsqrt(ms + EPS) * g_ref[...]).astype(BF16)
        xn_ref[...] = xn
        pif_ref[...] = _dot(xn, wif_ref[...])

    proj_ref[...] = _dot(xn_ref[...], w_ref[...])


def _in_proj(x, g, w_main, w_if, tm, tn):
    rows, d = x.shape
    n = w_main.shape[1]
    return pl.pallas_call(
        _in_proj_kernel,
        grid=(rows // tm, n // tn),
        in_specs=[pl.BlockSpec((tm, d), lambda i, j: (i, 0)),
                  pl.BlockSpec((1, d), lambda i, j: (0, 0)),
                  pl.BlockSpec((d, tn), lambda i, j: (0, j)),
                  pl.BlockSpec((d, LANE), lambda i, j: (0, 0))],
        out_specs=[pl.BlockSpec((tm, tn), lambda i, j: (i, j)),
                   pl.BlockSpec((tm, LANE), lambda i, j: (i, 0)),
                   pl.BlockSpec((tm, d), lambda i, j: (i, 0))],
        out_shape=[SDS((rows, n), F32), SDS((rows, LANE), F32), SDS((rows, d), BF16)],
        compiler_params=_params("parallel", "arbitrary"),
        name="in_proj",
    )(x, g, w_main, w_if)


def _gmlp_kernel(u_ref, v_ref, z_ref, lg_ref, lb_ref, ws_ref, bst_ref, y_ref, *vn_out, n_chunks):
    v = v_ref[...]
    vc = v - jnp.mean(v, axis=-1, keepdims=True)
    var = jnp.mean(vc * vc, axis=-1, keepdims=True)
    vn = vc * lax.rsqrt(var + EPS) * lg_ref[...] + lb_ref[...]
    if vn_out:
        vn_out[0][...] = vn
    vnb = vn.astype(BF16)
    row = lax.broadcasted_iota(jnp.int32, (CHUNK, CHUNK), 0)
    col = lax.broadcasted_iota(jnp.int32, (CHUNK, CHUNK), 1)
    for g in range(HEADS):
        gs = slice(g * LANE, (g + 1) * LANE)
        wm = jnp.where(row >= col, ws_ref[g], 0.0).astype(BF16)
        bcol = bst_ref[:, g:g + 1]
        for c in range(n_chunks):
            ts = slice(c * CHUNK, (c + 1) * CHUNK)
            s = _dot(wm, vnb[ts, gs]) + bcol
            y_ref[ts, gs] = (u_ref[ts, gs] * s * _silu(z_ref[ts, gs])).astype(BF16)


def _gmlp(proj3, ln_g, ln_b, ws, bs_t, tt, emit_vn):
    b, t, _ = proj3.shape
    w = ln_g.shape[-1]
    col = lambda c: pl.BlockSpec((None, tt, w), lambda i, j: (i, j, c))
    full = lambda a: pl.BlockSpec(a.shape, lambda i, j: (0,) * a.ndim)
    out_specs = [pl.BlockSpec((None, tt, w), lambda i, j: (i, j, 0))]
    out_shape = [SDS((b, t, w), BF16)]
    if emit_vn:
        out_specs.append(pl.BlockSpec((None, tt, w), lambda i, j: (i, j, 0)))
        out_shape.append(SDS((b, t, w), F32))
    return pl.pallas_call(
        functools.partial(_gmlp_kernel, n_chunks=tt // CHUNK),
        grid=(b, t // tt),
        in_specs=[col(0), col(1), col(2), full(ln_g), full(ln_b), full(ws), full(bs_t)],
        out_specs=out_specs,
        out_shape=out_shape,
        compiler_params=_params("parallel", "parallel"),
        name="gmlp",
    )(proj3, proj3, proj3, ln_g, ln_b, ws, bs_t)


def _causal_conv(buf_ref, x, cw_ref, cb_ref, tt):
    h0 = CONV_PAD - (CONV_W - 1)
    buf_ref[CONV_PAD:CONV_PAD + tt, :] = x
    y = cb_ref[...] + cw_ref[0:1, :] * buf_ref[h0:h0 + tt, :]
    for j in range(1, CONV_W):
        y = y + cw_ref[j:j + 1, :] * buf_ref[h0 + j:h0 + j + tt, :]
    buf_ref[h0:CONV_PAD, :] = x[tt - (CONV_W - 1):tt, :]
    return y


def _rglru_kernel(x_ref, z_ref, hist_ref, h0_ref, cw_ref, cb_ref, wa_ref, ba_ref, wx_ref, bx_ref, lam_ref,
                  y_ref, hist_out_ref, hlast_ref, buf_ref, hcar_ref, *, tt, t_valid, first_pos_zero):
    t = pl.program_id(1)
    h0 = CONV_PAD - (CONV_W - 1)

    @pl.when(t == 0)
    def _():
        buf_ref[h0:CONV_PAD, :] = hist_ref[...]
        hcar_ref[...] = h0_ref[...]

    x = x_ref[...]
    xc = _causal_conv(buf_ref, x, cw_ref, cb_ref, tt)
    xcb = xc.astype(BF16)
    ra, rx = [], []
    for blk in range(HEADS):
        bs = slice(blk * LANE, (blk + 1) * LANE)
        ra.append(_dot(xcb[:, bs], wa_ref[blk]))
        rx.append(_dot(xcb[:, bs], wx_ref[blk]))
    r = jax.nn.sigmoid(jnp.concatenate(ra, axis=-1) + ba_ref[...])
    i = jax.nn.sigmoid(jnp.concatenate(rx, axis=-1) + bx_ref[...])
    log_a = LRU_C * r * _log_sigmoid(lam_ref[...])
    a = jnp.exp(log_a)
    mult = jnp.sqrt(-jnp.tanh(log_a) * (a * a + 1.0))
    rowi = lax.broadcasted_iota(jnp.int32, (tt, 1), 0)
    if first_pos_zero:
        mult = jnp.where(rowi + t * tt == 0, 1.0, mult)
    u = mult * (i * xc)
    d = 1
    while d < tt:
        keep = rowi >= d
        a_sh = jnp.where(keep, pltpu.roll(a, d, axis=0), 1.0)
        u_sh = jnp.where(keep, pltpu.roll(u, d, axis=0), 0.0)
        u = a * u_sh + u
        a = a * a_sh
        d *= 2
    h = a * hcar_ref[...] + u
    hcar_ref[...] = h[tt - 1:tt, :]
    y_ref[...] = (h * _silu(z_ref[...])).astype(BF16)

    @pl.when(t == pl.num_programs(1) - 1)
    def _():
        hist_out_ref[...] = x[t_valid - (CONV_W - 1):t_valid, :]
        hlast_ref[...] = h[t_valid - 1:t_valid, :]


def _rglru(proj3, hist, h0, cw, cb, wa, ba, wx, bx, lam, tt, t_valid, first_pos_zero):
    b, t, _ = proj3.shape
    w = cb.shape[-1]
    assert t_valid >= CONV_W - 1 and (t_valid == tt or t == tt)
    col = lambda c: pl.BlockSpec((None, tt, w), lambda i, j: (i, j, c))
    full = lambda a: pl.BlockSpec(a.shape, lambda i, j: (0,) * a.ndim)
    per_b = lambda r: pl.BlockSpec((None, r, w), lambda i, j: (i, 0, 0))
    return pl.pallas_call(
        functools.partial(_rglru_kernel, tt=tt, t_valid=t_valid, first_pos_zero=first_pos_zero),
        grid=(b, t // tt),
        in_specs=[col(3), col(4), per_b(CONV_W - 1), per_b(1)]
                 + [full(a) for a in (cw, cb, wa, ba, wx, bx, lam)],
        out_specs=[pl.BlockSpec((None, tt, w), lambda i, j: (i, j, 0)), per_b(CONV_W - 1), per_b(1)],
        out_shape=[SDS((b, t, w), BF16), SDS((b, CONV_W - 1, w), F32), SDS((b, 1, w), F32)],
        scratch_shapes=[pltpu.VMEM((CONV_PAD + tt, w), F32), pltpu.VMEM((1, w), F32)],
        compiler_params=_params("parallel", "arbitrary"),
        name="rglru",
    )(proj3, proj3, hist, h0, cw, cb, wa, ba, wx, bx, lam)


def _later_matrix(n):
    row = lax.broadcasted_iota(jnp.int32, (n, n), 0)
    col = lax.broadcasted_iota(jnp.int32, (n, n), 1)
    return (row > col).astype(BF16)


def _sb_logits(z):
    log_b = jnp.minimum(z, 0.0) - jnp.log(1.0 + jnp.exp(-jnp.abs(z)))
    return log_b, log_b - z


def _sb_span(qb, kspan, vspan, r, later, bias, scale, mask):
    cb = later.shape[0]
    log_b, log_1mb = _sb_logits(_dot_nt(qb, kspan) * scale + bias)
    if mask is not None:
        log_1mb = jnp.where(mask, log_1mb, 0.0)
    l1b = log_1mb.astype(BF16)
    atts = []
    for j in reversed(range(kspan.shape[0] // cb)):
        js = slice(j * cb, (j + 1) * cb)
        att = jnp.exp(log_b[:, js] + _dot(l1b[:, js], later) + r)
        if mask is not None:
            att = jnp.where(mask[:, js], att, 0.0)
        atts.insert(0, att.astype(BF16))
        r = r + jnp.sum(log_1mb[:, js], axis=-1, keepdims=True)
    return _dot(jnp.concatenate(atts, axis=-1), vspan), r


def _sb_attn_kernel(bias_ref, q_ref, k_ref, v_ref, z_ref, y_ref, kb_ref, vb_ref, acc_ref, r_ref, *, tq, cb, scale):
    qi = pl.program_id(2)

    @pl.when(qi == 0)
    def _():
        kb_ref[...] = k_ref[...].astype(BF16)
        vb_ref[...] = v_ref[...].astype(BF16)

    bias = bias_ref[pl.program_id(1)]
    later = _later_matrix(cb)
    qb = q_ref[...].astype(BF16)

    def span(k0, r, mask):
        return _sb_span(qb, kb_ref[pl.ds(k0, tq), :], vb_ref[pl.ds(k0, tq), :], r, later, bias, scale, mask)

    rr = lax.broadcasted_iota(jnp.int32, (tq, tq), 0)
    cc = lax.broadcasted_iota(jnp.int32, (tq, tq), 1)
    acc_ref[...], r_ref[...] = span(pl.multiple_of(qi * tq, tq), jnp.zeros((tq, 1), F32), cc < rr)

    def body(i, carry):
        pv, r_ref[...] = span(pl.multiple_of((qi - 1 - i) * tq, tq), r_ref[...], None)
        acc_ref[...] += pv
        return carry

    lax.fori_loop(0, qi, body, 0)
    y_ref[...] = (acc_ref[...] * _silu(z_ref[...])).astype(BF16)


def _sb_attn(proj3, bias, tq):
    b, t, _ = proj3.shape
    hd = LANE
    cb = min(tq, 2 * CHUNK)
    qcol, kcol, vcol, zcol = (5 * HEADS, 6 * HEADS, 7 * HEADS, 8 * HEADS)
    tile = lambda c0: pl.BlockSpec((None, tq, hd), lambda i, h, j: (i, j, c0 + h))
    whole = lambda c0: pl.BlockSpec((None, t, hd), lambda i, h, j: (i, 0, c0 + h))
    return pl.pallas_call(
        functools.partial(_sb_attn_kernel, tq=tq, cb=cb, scale=hd ** -0.5),
        grid=(b, HEADS, t // tq),
        in_specs=[pl.BlockSpec(memory_space=pltpu.SMEM), tile(qcol), whole(kcol), whole(vcol), tile(zcol)],
        out_specs=pl.BlockSpec((None, tq, hd), lambda i, h, j: (i, j, h)),
        out_shape=SDS((b, t, HEADS * hd), BF16),
        scratch_shapes=[pltpu.VMEM((t, hd), BF16), pltpu.VMEM((t, hd), BF16),
                        pltpu.VMEM((tq, hd), F32), pltpu.VMEM((tq, 1), F32)],
        compiler_params=_params("parallel", "parallel", "arbitrary"),
        name="sb_attn",
    )(bias, proj3, proj3, proj3, proj3)


def _sb_decode_kernel(pt_ref, bias_ref, q_ref, kn_ref, vn_ref, z_ref, *rest, n_pages, scale):
    k_pages, v_pages = rest[:n_pages], rest[n_pages:2 * n_pages]
    y_ref, acc_ref, r_ref = rest[2 * n_pages:]
    s = pl.program_id(1)
    later = _later_matrix(CHUNK)

    def update(pieces, masked):
        zs = [_dot_nt(q_ref[:, h * LANE:(h + 1) * LANE].astype(BF16), kblk.astype(BF16)) * scale + bias_ref[h]
              for h, kblk, _ in pieces]
        log_b, log_1mb = _sb_logits(jnp.concatenate(zs, axis=0))
        if masked:
            rr = lax.broadcasted_iota(jnp.int32, log_b.shape, 0) & (DEC_ROWS - 1)
            mask = lax.broadcasted_iota(jnp.int32, log_b.shape, 1) < rr
            log_1mb = jnp.where(mask, log_1mb, 0.0)
        suffix = _dot(log_1mb.astype(BF16), later)
        total = jnp.sum(log_1mb, axis=-1, keepdims=True)
        run = [r_ref[h] for h in range(HEADS)]
        acc = [acc_ref[h] for h in range(HEADS)]
        for i, (h, _, vblk) in enumerate(pieces):
            rows = slice(i * DEC_ROWS, (i + 1) * DEC_ROWS)
            att = jnp.exp(log_b[rows] + suffix[rows] + run[h])
            if masked:
                att = jnp.where(mask[rows], att, 0.0)
            acc[h] = acc[h] + _dot(att.astype(BF16), vblk.astype(BF16))
            run[h] = run[h] + total[rows]
        for h in range(HEADS):
            r_ref[h] = run[h]
            acc_ref[h] = acc[h]

    @pl.when(s == 0)
    def _():
        acc_ref[...] = jnp.zeros_like(acc_ref)
        r_ref[...] = jnp.zeros_like(r_ref)
        update([(h, kn_ref[:, h * LANE:(h + 1) * LANE], vn_ref[:, h * LANE:(h + 1) * LANE]) for h in range(HEADS)],
               True)

    update([(h, k_pages[g][pl.ds(h, CHUNK, stride=HEADS), :], v_pages[g][pl.ds(h, CHUNK, stride=HEADS), :])
            for g in range(n_pages) for h in range(HEADS)], False)

    @pl.when(s == pl.num_programs(1) - 1)
    def _():
        acc = jnp.concatenate([acc_ref[h] for h in range(HEADS)], axis=-1)
        y_ref[...] = (acc * _silu(z_ref[...])).astype(BF16)


def _sb_decode(page_table, bias, q, k_new, v_new, z, cache_k, cache_v, layer, pages_per_step):
    b, n_tab = page_table.shape
    w = q.shape[-1]
    g = pages_per_step
    assert n_tab % g == 0
    page_rows, hd = cache_k.shape[2:]
    per_b = lambda r: pl.BlockSpec((None, r, w), lambda i, s, pt, bs: (i, 0, 0))

    def page_spec(j):
        return pl.BlockSpec((None, None, page_rows, hd),
                            lambda i, s, pt, bs: (layer, pt[i, n_tab - 1 - (s * g + j)], 0, 0))

    grid_spec = pltpu.PrefetchScalarGridSpec(
        num_scalar_prefetch=2,
        grid=(b, n_tab // g),
        in_specs=[per_b(DEC_ROWS), per_b(CHUNK), per_b(CHUNK), per_b(DEC_ROWS)]
                 + [page_spec(j) for j in range(g)] * 2,
        out_specs=per_b(DEC_ROWS),
        scratch_shapes=[pltpu.VMEM((HEADS, DEC_ROWS, hd), F32), pltpu.VMEM((HEADS, DEC_ROWS, hd), F32)],
    )
    return pl.pallas_call(
        functools.partial(_sb_decode_kernel, n_pages=g, scale=hd ** -0.5),
        grid_spec=grid_spec,
        out_shape=SDS((b, DEC_ROWS, w), BF16),
        compiler_params=_params("parallel", "arbitrary"),
        name="sb_decode",
    )(page_table, bias, q, k_new, v_new, z, *([cache_k] * g), *([cache_v] * g))


def _mlstm_kernel(x_ref, z_ref, o_ref, gc_ref, gr_ref, hist_ref, c0_ref, n0_ref, m0_ref,
                  cw_ref, cb_ref, wq_ref, wk_ref, wv_ref, brow_ref, bcol_ref, ng_ref, sk_ref,
                  y_ref, hist_out_ref, c_out_ref, n_out_ref, m_out_ref,
                  buf_ref, c_scr, n_scr, m_scr, *, t_valid, scale):
    c = pl.program_id(1)
    L = CHUNK
    h0 = CONV_PAD - (CONV_W - 1)

    @pl.when(c == 0)
    def _():
        buf_ref[h0:CONV_PAD, :] = hist_ref[...]
        c_scr[...] = c0_ref[...]
        n_scr[...] = n0_ref[...]
        m_scr[...] = m0_ref[...]

    x = x_ref[...]
    xc = _silu(_causal_conv(buf_ref, x, cw_ref, cb_ref, L))
    xcb = xc.astype(BF16)
    xb = x.astype(BF16)
    gc = gc_ref[...] + brow_ref[...]
    gr = gr_ref[...] + bcol_ref[...]
    row = lax.broadcasted_iota(jnp.int32, (L, L), 0)
    col = lax.broadcasted_iota(jnp.int32, (L, L), 1)
    causal = row >= col
    bc_all = _cumsum_dot_left(causal.astype(BF16), _log_sigmoid(gc))
    br_all = _cumsum_dot_right(_log_sigmoid(gr), (row <= col).astype(BF16))
    rowi = lax.broadcasted_iota(jnp.int32, (L, 1), 0)
    outs = []
    for h in range(HEADS):
        hs = slice(h * LANE, (h + 1) * LANE)
        q = _dot(xcb[:, hs], wq_ref[h])
        k = _dot(xcb[:, hs], wk_ref[h]) * scale
        vb = _dot(xb[:, hs], wv_ref[h]).astype(BF16)
        qb = q.astype(BF16)
        ig_c, b_c = gc[:, h:h + 1], bc_all[:, HEADS + h:HEADS + h + 1]
        ig_r, b_r = gr[h:h + 1, :], br_all[HEADS + h:HEADS + h + 1, :]
        m_h = m_scr[h:h + 1, 0:1]
        c_h = c_scr[h]
        n_h = n_scr[h:h + 1, :]
        dmat = jnp.where(causal, b_c - b_r + ig_r, -jnp.inf)
        inter = b_c + m_h
        m_t = jnp.maximum(jnp.max(dmat, axis=-1, keepdims=True), inter)
        s = _dot_nt(qb, k.astype(BF16)) * jnp.exp(dmat - m_t)
        w_inter = jnp.exp(inter - m_t)
        num = _dot(s.astype(BF16), vb) + w_inter * _dot(qb, c_h.astype(BF16))
        den = jnp.sum(s, axis=-1, keepdims=True) + w_inter * jnp.sum(q * n_h, axis=-1, keepdims=True)
        hh = num / jnp.maximum(jnp.abs(den), jnp.exp(-m_t))
        b_last = b_c[t_valid - 1:t_valid, :]
        g = b_last - b_c + ig_c
        if t_valid < L:
            g = jnp.where(rowi < t_valid, g, -jnp.inf)
        m_new = jnp.maximum(b_last + m_h, jnp.max(g, axis=0, keepdims=True))
        kw = k * jnp.exp(g - m_new)
        decay = jnp.exp(b_last + m_h - m_new)
        c_scr[h] = decay * c_h + _dot(kw.T.astype(BF16), vb)
        n_scr[h:h + 1, :] = decay * n_h + jnp.sum(kw, axis=0, keepdims=True)
        m_scr[h:h + 1, :] = jnp.broadcast_to(m_new, (1, LANE))
        hh = jax.nn.sigmoid(o_ref[:, hs]) * hh
        hc = hh - jnp.mean(hh, axis=-1, keepdims=True)
        outs.append(hc * lax.rsqrt(jnp.mean(hc * hc, axis=-1, keepdims=True) + EPS))
    hn = jnp.concatenate(outs, axis=-1) * ng_ref[...] + sk_ref[...] * xc
    y_ref[...] = (hn * _silu(z_ref[...])).astype(BF16)

    @pl.when(c == pl.num_programs(1) - 1)
    def _():
        hist_out_ref[...] = x[t_valid - (CONV_W - 1):t_valid, :]
        c_out_ref[...] = c_scr[...]
        n_out_ref[...] = n_scr[...]
        m_out_ref[...] = m_scr[...]


def _mlstm(proj3, gates_col, gates_row, hist, c0, n0, m0, cw, cb, wq, wk, wv, b_row, b_col, ng, sk, t_valid):
    b, t, _ = proj3.shape
    w = cb.shape[-1]
    hd = w // HEADS
    L = CHUNK
    assert t_valid >= CONV_W - 1 and (t_valid == L or t == L)
    col = lambda c: pl.BlockSpec((None, L, w), lambda i, j: (i, j, c))
    full = lambda a: pl.BlockSpec(a.shape, lambda i, j: (0,) * a.ndim)
    per_b = lambda *s: pl.BlockSpec((None,) + s, lambda i, j: (i,) + (0,) * len(s))
    return pl.pallas_call(
        functools.partial(_mlstm_kernel, t_valid=t_valid, scale=hd ** -0.5),
        grid=(b, t // L),
        in_specs=[col(9), col(10), col(11),
                  pl.BlockSpec((None, L, LANE), lambda i, j: (i, j, 0)),
                  pl.BlockSpec((None, 2 * HEADS, L), lambda i, j: (i, 0, j)),
                  per_b(CONV_W - 1, w), per_b(HEADS, hd, hd), per_b(HEADS, hd), per_b(HEADS, LANE)]
                 + [full(a) for a in (cw, cb, wq, wk, wv, b_row, b_col, ng, sk)],
        out_specs=[pl.BlockSpec((None, L, w), lambda i, j: (i, j, 0)),
                   per_b(CONV_W - 1, w), per_b(HEADS, hd, hd), per_b(HEADS, hd), per_b(HEADS, LANE)],
        out_shape=[SDS((b, t, w), BF16), SDS((b, CONV_W - 1, w), F32), SDS((b, HEADS, hd, hd), F32),
                   SDS((b, HEADS, hd), F32), SDS((b, HEADS, LANE), F32)],
        scratch_shapes=[pltpu.VMEM((CONV_PAD + L, w), F32), pltpu.VMEM((HEADS, hd, hd), F32),
                        pltpu.VMEM((HEADS, hd), F32), pltpu.VMEM((HEADS, LANE), F32)],
        compiler_params=_params("parallel", "arbitrary"),
        name="mlstm",
    )(proj3, proj3, proj3, gates_col, gates_row, hist, c0, n0, m0, cw, cb, wq, wk, wv, b_row, b_col, ng, sk)


def _merge_kernel(xn_ref, a_ref, b_ref, c_ref, d_ref, wg0_ref, wg1_ref, wg2_ref, wg3_ref, bg_ref, wb_ref, o_ref):
    xn = xn_ref[...]
    acc = None
    branches = (a_ref, b_ref, c_ref, d_ref)
    gates = (wg0_ref, wg1_ref, wg2_ref, wg3_ref)
    for m in range(N_BRANCH):
        gate = jax.nn.sigmoid(_dot(xn, gates[m][...]) + bg_ref[m:m + 1, :])
        term = gate * _dot(branches[m][...], wb_ref[m])
        acc = term if acc is None else acc + term
    o_ref[...] = acc.astype(BF16)


def _merge(xn, branches, w_gate, b_gate, w_branch, tm, tn):
    rows, d = xn.shape
    w = branches[0].shape[-1]
    nj = d // tn
    gate_spec = lambda m: pl.BlockSpec((d, tn), lambda i, j: (0, m * nj + j))
    return pl.pallas_call(
        _merge_kernel,
        grid=(rows // tm, nj),
        in_specs=[pl.BlockSpec((tm, d), lambda i, j: (i, 0))]
                 + [pl.BlockSpec((tm, w), lambda i, j: (i, 0))] * N_BRANCH
                 + [gate_spec(m) for m in range(N_BRANCH)]
                 + [pl.BlockSpec((N_BRANCH, tn), lambda i, j: (0, j)),
                    pl.BlockSpec((N_BRANCH, w, tn), lambda i, j: (0, 0, j))],
        out_specs=pl.BlockSpec((tm, tn), lambda i, j: (i, j)),
        out_shape=SDS((rows, d), BF16),
        compiler_params=_params("parallel", "arbitrary"),
        name="merge",
    )(xn, *branches, w_gate, w_gate, w_gate, w_gate, b_gate, w_branch)


def _out_proj_kernel(m_ref, x_ref, w_ref, g_ref, o_ref):
    out = _dot(m_ref[...], w_ref[...])
    ms = jnp.mean(out * out, axis=-1, keepdims=True)
    o_ref[...] = x_ref[...] + out * lax.rsqrt(ms + EPS) * g_ref[...]


def _out_proj(merged, x, w_out, g, tm):
    rows, d = x.shape
    return pl.pallas_call(
        _out_proj_kernel,
        grid=(rows // tm,),
        in_specs=[pl.BlockSpec((tm, d), lambda i: (i, 0)),
                  pl.BlockSpec((tm, d), lambda i: (i, 0)),
                  pl.BlockSpec((d, d), lambda i: (0, 0)),
                  pl.BlockSpec((1, d), lambda i: (0, 0))],
        out_specs=pl.BlockSpec((tm, d), lambda i: (i, 0)),
        out_shape=SDS((rows, d), F32),
        compiler_params=_params("parallel"),
        name="out_proj",
    )(merged, x, w_out, g)


def _tile(n, pref):
    return pref if n % pref == 0 else n


def _branches(proj3, gates_col, P, st, t_valid, first_pos_zero, emit_vn):
    b, t, _ = proj3.shape
    gm = _gmlp(proj3, P['ln_g'], P['ln_b'], P['ws'], P['bs_t'], _tile(t, 4 * CHUNK), emit_vn)
    tt = _tile(t, 4 * CHUNK) if t_valid == CHUNK else t
    y_b, hist_b, h_b = _rglru(proj3, st['conv_b'], st['h_b'], P['lru_cw'], P['lru_cb'], P['lru_wa'], P['lru_ba'],
                              P['lru_wx'], P['lru_bx'], P['lru_lam'], tt, tt if t_valid == CHUNK else t_valid,
                              first_pos_zero)
    gates_row = jnp.swapaxes(gates_col[:, :, :2 * HEADS], 1, 2)
    y_d, hist_d, c, n, m = _mlstm(proj3, gates_col, gates_row, st['conv_d'], st['C'], st['n'], st['m'],
                                  P['ml_cw'], P['ml_cb'], P['ml_wq'], P['ml_wk'], P['ml_wv'], P['ml_brow'],
                                  P['ml_bcol'], P['ml_ng'], P['ml_sk'], t_valid)
    new_st = dict(conv_b=hist_b, h_b=h_b[:, 0], conv_d=hist_d, C=c, n=n, m=m[:, :, 0])
    return gm, y_b, y_d, new_st


def _dense_tail(x, xn, branches, P, tm):
    merged = _merge(xn, branches, P['w_gate'], P['b_gate'], P['w_branch'], tm, _tile(x.shape[1], 256))
    return _out_proj(merged, x, P['w_out'], P['norm_post'], _tile(x.shape[0], 256))


def _state_in(conv_b, h_b, conv_d, c, n, m):
    return dict(conv_b=conv_b, h_b=h_b[:, None, :], conv_d=conv_d, C=c, n=n,
                m=jnp.broadcast_to(m[..., None], m.shape + (LANE,)))


def kernel(x_prompt, x_sample, cache_k, cache_v, page_table, state_rglru_conv, state_rglru_h, state_mlstm_conv, state_mlstm_c, state_mlstm_n, state_mlstm_m, norm_pre, norm_post, w_in, gmlp_ln_g, gmlp_ln_b, gmlp_ws, gmlp_bs, lru_conv_w, lru_conv_b, lru_wa, lru_ba, lru_wx, lru_bx, lru_lambda, ml_conv_w, ml_conv_b, ml_wq, ml_wk, ml_wv, ml_bi, ml_bf, ml_norm_g, ml_skip, sb_bias, w_branch, w_gate, b_gate, w_out):
    bp, tp, d = x_prompt.shape
    bs, ts, _ = x_sample.shape
    depth = w_in.shape[0]
    w = d // N_BRANCH
    hd = w // HEADS
    n_main = 12 * w
    page = cache_k.shape[2]
    past_len = page_table.shape[1] * page
    assert tp % CHUNK == 0 and page == CHUNK and ts <= DEC_ROWS and cache_k.shape[3:] == (HEADS, hd)
    ck = cache_k.reshape(depth, cache_k.shape[1], page * HEADS, hd)
    cv = cache_v.reshape(depth, cache_v.shape[1], page * HEADS, hd)
    pages_per_step = 8 if page_table.shape[1] % 8 == 0 else 1

    xp = x_prompt.reshape(bp * tp, d)
    xs = x_sample.reshape(bs * ts, d)
    zeros_p = _state_in(jnp.zeros((bp, CONV_W - 1, w), F32), jnp.zeros((bp, w), F32),
                        jnp.zeros((bp, CONV_W - 1, w), F32), jnp.zeros((bp, HEADS, hd, hd), F32),
                        jnp.zeros((bp, HEADS, hd), F32), jnp.zeros((bp, HEADS), F32))
    names = ('conv_b', 'h_b', 'conv_d', 'C', 'n', 'm')
    res_p = {k: [] for k in names}
    res_s = {k: [] for k in names}
    kp_l, vp_l, ks_l, vs_l, gv_l = [], [], [], [], []
    row2 = lambda a: a[None, :]

    for l in range(depth):
        w_if = jnp.pad(w_in[l, :, n_main:], ((0, 0), (0, LANE - 2 * HEADS))).astype(BF16)
        gate_bias = jnp.concatenate([ml_bi[l], ml_bf[l]])
        P = dict(
            ln_g=row2(gmlp_ln_g[l]), ln_b=row2(gmlp_ln_b[l]), ws=gmlp_ws[l], bs_t=gmlp_bs[l].T,
            lru_cw=lru_conv_w[l], lru_cb=row2(lru_conv_b[l]), lru_wa=lru_wa[l].astype(BF16), lru_ba=row2(lru_ba[l]),
            lru_wx=lru_wx[l].astype(BF16), lru_bx=row2(lru_bx[l]), lru_lam=row2(lru_lambda[l]),
            ml_cw=ml_conv_w[l], ml_cb=row2(ml_conv_b[l]), ml_wq=ml_wq[l].astype(BF16), ml_wk=ml_wk[l].astype(BF16),
            ml_wv=ml_wv[l].astype(BF16), ml_brow=jnp.pad(gate_bias, (0, LANE - 2 * HEADS))[None, :],
            ml_bcol=gate_bias[:, None], ml_ng=row2(ml_norm_g[l]), ml_sk=row2(ml_skip[l]),
            w_gate=w_gate[l].astype(BF16), b_gate=b_gate[l].reshape(N_BRANCH, d),
            w_branch=w_branch[l].astype(BF16), w_out=w_out[l].astype(BF16), norm_post=row2(norm_post[l]))
        w_main = w_in[l, :, :n_main].astype(BF16)
        g_pre = row2(norm_pre[l])

        proj, pif, xn = _in_proj(xp, g_pre, w_main, w_if, _tile(bp * tp, 1024), 512)
        proj3 = proj.reshape(bp, tp, n_main)
        y_a, y_b, y_d, nst = _branches(proj3, pif.reshape(bp, tp, LANE), P, zeros_p, CHUNK, True, False)
        y_c = _sb_attn(proj3, sb_bias[l], _tile(tp, 4 * CHUNK))
        xp = _dense_tail(xp, xn, [y.reshape(bp * tp, w) for y in (y_a[0], y_b, y_c, y_d)], P, _tile(bp * tp, 1024))
        for k in names:
            res_p[k].append(nst[k])
        kp_l.append(proj3[:, :, 6 * w:7 * w].reshape(bp, tp, HEADS, hd))
        vp_l.append(proj3[:, :, 7 * w:8 * w].reshape(bp, tp, HEADS, hd))

        proj, pif, xn = _in_proj(xs, g_pre, w_main, w_if, bs * ts, 512)
        pad_t = lambda a, rows: jnp.pad(a, ((0, 0), (0, rows - ts), (0, 0)))
        proj3 = proj.reshape(bs, ts, n_main)
        projc = pad_t(proj3, CHUNK)
        st_s = _state_in(state_rglru_conv[l], state_rglru_h[l], state_mlstm_conv[l], state_mlstm_c[l],
                         state_mlstm_n[l], state_mlstm_m[l])
        (y_a, vn), y_b, y_d, nst = _branches(projc, pad_t(pif.reshape(bs, ts, LANE), CHUNK), P, st_s, ts,
                                             past_len == 0, True)
        y_c = _sb_decode(page_table, sb_bias[l], pad_t(proj3[:, :, 5 * w:6 * w], DEC_ROWS),
                         projc[:, :, 6 * w:7 * w], projc[:, :, 7 * w:8 * w], pad_t(proj3[:, :, 8 * w:9 * w], DEC_ROWS),
                         ck, cv, l, pages_per_step)
        xs = _dense_tail(xs, xn, [y[:, :ts].reshape(bs * ts, w) for y in (y_a, y_b, y_c, y_d)], P, bs * ts)
        for k in names:
            res_s[k].append(nst[k])
        ks_l.append(proj3[:, :, 6 * w:7 * w].reshape(bs, ts, HEADS, hd))
        vs_l.append(proj3[:, :, 7 * w:8 * w].reshape(bs, ts, HEADS, hd))
        gv_l.append(vn[:, :ts])

    st = lambda lst: jnp.stack(lst, axis=0)
    return (xp.reshape(bp, tp, d), xs.reshape(bs, ts, d), st(kp_l), st(vp_l), st(ks_l), st(vs_l),
            st(res_p['conv_b']), st(res_p['h_b']), st(res_s['conv_b']), st(res_s['h_b']),
            st(res_p['conv_d']), st(res_p['C']), st(res_p['n']), st(res_p['m']),
            st(res_s['conv_d']), st(res_s['C']), st(res_s['n']), st(res_s['m']),
            st(gv_l))
```

```python
import functools

import jax
import jax.numpy as jnp
from jax import lax
from jax.experimental import pallas as pl
from jax.experimental.pallas import tpu as pltpu

F32 = jnp.float32
BF16 = jnp.bfloat16
SDS = jax.ShapeDtypeStruct

EPS = 1e-6
N_BRANCH = 4
HEADS = 4
CONV_W = 4
CHUNK = 128
LRU_C = 8.0
LANE = 128
SUBLANE = 8
CONV_PAD = SUBLANE
DEC_ROWS = 16
K_GROUP, V_GROUP = 6, 7
VMEM_LIMIT = 56 * 1024 * 1024


def _log_sigmoid(z):
    return jnp.minimum(z, 0.0) - jnp.log1p(jnp.exp(-jnp.abs(z)))


def _silu(x):
    return x * jax.nn.sigmoid(x)


def _dot(a, b):
    return jnp.dot(a, b, preferred_element_type=F32)


def _dot_nt(a, b):
    return lax.dot_general(a, b, (((1,), (1,)), ((), ())), preferred_element_type=F32)


def _split(x):
    hi = x.astype(BF16)
    return hi, (x - hi.astype(F32)).astype(BF16)


def _cumsum_dot_right(x, m):
    hi, lo = _split(x)
    return _dot(hi, m) + _dot(lo, m)


def _cumsum_dot_left(m, x):
    hi, lo = _split(x)
    return _dot(m, hi) + _dot(m, lo)


def _params(*sem):
    return pltpu.CompilerParams(dimension_semantics=sem, vmem_limit_bytes=VMEM_LIMIT)


def _layer_spec(a, l):
    return pl.BlockSpec((None,) + a.shape[1:], lambda *_: (l,) + (0,) * (a.ndim - 1))


def _in_proj_kernel(x_ref, g_ref, w_ref, wif_ref, *rest, kv_out):
    if kv_out:
        proj_ref, pif_ref, xn_ref, k_ref, v_ref = rest[-5:]
    else:
        proj_ref, pif_ref, xn_ref = rest
    j = pl.program_id(1)

    @pl.when(j == 0)
    def _():
        x = x_ref[...]
        ms = jnp.mean(x * x, axis=-1, keepdims=True)
        xn = (x * lax.rsqrt(ms + EPS) * g_ref[...]).astype(BF16)
        xn_ref[...] = xn
        pif_ref[...] = _dot(xn, wif_ref[...])

    proj = _dot(xn_ref[...], w_ref[...])
    proj_ref[...] = proj
    if kv_out:
        tm = proj.shape[0]
        for grp, ref in ((K_GROUP, k_ref), (V_GROUP, v_ref)):
            @pl.when(j == grp)
            def _(ref=ref):
                for h in range(HEADS):
                    ref[pl.ds(h, tm, stride=HEADS), :] = proj[:, h * LANE:(h + 1) * LANE]


def _in_proj(x, g, w_in, w_if, l, tm, kv_prev=None, kv_out=False):
    rows, d = x.shape
    depth = w_in.shape[0]
    tn = d // N_BRANCH
    n = 12 * tn
    in_specs = [pl.BlockSpec((tm, d), lambda i, j: (i, 0)),
                _layer_spec(g, l),
                pl.BlockSpec((None, d, tn), lambda i, j: (l, 0, j)),
                _layer_spec(w_if, l)]
    out_specs = [pl.BlockSpec((tm, tn), lambda i, j: (i, j)),
                 pl.BlockSpec((tm, LANE), lambda i, j: (i, 0)),
                 pl.BlockSpec((tm, d), lambda i, j: (i, 0))]
    out_shape = [SDS((rows, n), F32), SDS((rows, LANE), F32), SDS((rows, d), BF16)]
    args = [x, g, w_in, w_if]
    aliases = {}
    if kv_out:
        kv_spec = pl.BlockSpec((None, tm * HEADS, LANE), lambda i, j: (l, i, 0))
        out_specs += [kv_spec, kv_spec]
        out_shape += [SDS((depth, rows * HEADS, LANE), F32)] * 2
        if kv_prev is not None:
            in_specs += [pl.BlockSpec(memory_space=pl.ANY)] * 2
            args += list(kv_prev)
            aliases = {4: 3, 5: 4}
    return pl.pallas_call(
        functools.partial(_in_proj_kernel, kv_out=kv_out),
        grid=(rows // tm, n // tn),
        in_specs=in_specs,
        out_specs=out_specs,
        out_shape=out_shape,
        input_output_aliases=aliases,
        compiler_params=_params("parallel", "arbitrary"),
        name="in_proj",
    )(*args)


def _gmlp_kernel(u_ref, v_ref, z_ref, lg_ref, lb_ref, ws_ref, bst_ref, y_ref, *vn_out, n_chunks):
    v = v_ref[...]
    vc = v - jnp.mean(v, axis=-1, keepdims=True)
    var = jnp.mean(vc * vc, axis=-1, keepdims=True)
    vn = vc * lax.rsqrt(var + EPS) * lg_ref[...] + lb_ref[...]
    if vn_out:
        vn_out[0][...] = vn
    vnb = vn.astype(BF16)
    row = lax.broadcasted_iota(jnp.int32, (CHUNK, CHUNK), 0)
    col = lax.broadcasted_iota(jnp.int32, (CHUNK, CHUNK), 1)
    for g in range(HEADS):
        gs = slice(g * LANE, (g + 1) * LANE)
        wm = jnp.where(row >= col, ws_ref[g], 0.0).astype(BF16)
        bcol = bst_ref[:, g:g + 1]
        for c in range(n_chunks):
            ts = slice(c * CHUNK, (c + 1) * CHUNK)
            s = _dot(wm, vnb[ts, gs]) + bcol
            y_ref[ts, gs] = (u_ref[ts, gs] * s * _silu(z_ref[ts, gs])).astype(BF16)


def _gmlp(proj3, P, l, tt, emit_vn):
    b, t, _ = proj3.shape
    w = P['ln_g'].shape[-1]
    col = lambda c: pl.BlockSpec((None, tt, w), lambda i, j: (i, j, c))
    out_specs = [pl.BlockSpec((None, tt, w), lambda i, j: (i, j, 0))]
    out_shape = [SDS((b, t, w), BF16)]
    if emit_vn:
        out_specs.append(pl.BlockSpec((None, tt, w), lambda i, j: (i, j, 0)))
        out_shape.append(SDS((b, t, w), F32))
    params = [P[k] for k in ('ln_g', 'ln_b', 'ws', 'bs_t')]
    return pl.pallas_call(
        functools.partial(_gmlp_kernel, n_chunks=tt // CHUNK),
        grid=(b, t // tt),
        in_specs=[col(0), col(1), col(2)] + [_layer_spec(a, l) for a in params],
        out_specs=out_specs,
        out_shape=out_shape,
        compiler_params=_params("parallel", "parallel"),
        name="gmlp",
    )(proj3, proj3, proj3, *params)


def _causal_conv(buf_ref, x, cw_ref, cb_ref, tt):
    h0 = CONV_PAD - (CONV_W - 1)
    buf_ref[CONV_PAD:CONV_PAD + tt, :] = x
    y = cb_ref[...] + cw_ref[0:1, :] * buf_ref[h0:h0 + tt, :]
    for j in range(1, CONV_W):
        y = y + cw_ref[j:j + 1, :] * buf_ref[h0 + j:h0 + j + tt, :]
    buf_ref[h0:CONV_PAD, :] = x[tt - (CONV_W - 1):tt, :]
    return y


def _state_spec(a, sl):
    return pl.BlockSpec((None, None) + a.shape[2:], lambda i, j: (sl, i) + (0,) * (a.ndim - 2))


def _rglru_kernel(x_ref, z_ref, hist_ref, h0_ref, cw_ref, cb_ref, wa_ref, ba_ref, wx_ref, bx_ref, lam_ref,
                  y_ref, hist_out_ref, hlast_ref, buf_ref, hcar_ref, *, tt, t_valid, first_pos_zero):
    t = pl.program_id(1)
    h0 = CONV_PAD - (CONV_W - 1)

    @pl.when(t == 0)
    def _():
        buf_ref[h0:CONV_PAD, :] = hist_ref[...]
        hcar_ref[...] = h0_ref[...]

    x = x_ref[...]
    xc = _causal_conv(buf_ref, x, cw_ref, cb_ref, tt)
    xcb = xc.astype(BF16)
    ra, rx = [], []
    for blk in range(HEADS):
        bs = slice(blk * LANE, (blk + 1) * LANE)
        ra.append(_dot(xcb[:, bs], wa_ref[blk]))
        rx.append(_dot(xcb[:, bs], wx_ref[blk]))
    r = jax.nn.sigmoid(jnp.concatenate(ra, axis=-1) + ba_ref[...])
    i = jax.nn.sigmoid(jnp.concatenate(rx, axis=-1) + bx_ref[...])
    log_a = LRU_C * r * _log_sigmoid(lam_ref[...])
    a = jnp.exp(log_a)
    mult = jnp.sqrt(-jnp.tanh(log_a) * (a * a + 1.0))
    rowi = lax.broadcasted_iota(jnp.int32, (tt, 1), 0)
    if first_pos_zero:
        mult = jnp.where(rowi + t * tt == 0, 1.0, mult)
    u = mult * (i * xc)
    d = 1
    while d < tt:
        keep = rowi >= d
        a_sh = jnp.where(keep, pltpu.roll(a, d, axis=0), 1.0)
        u_sh = jnp.where(keep, pltpu.roll(u, d, axis=0), 0.0)
        u = a * u_sh + u
        a = a * a_sh
        d *= 2
    h = a * hcar_ref[...] + u
    hcar_ref[...] = h[tt - 1:tt, :]
    y_ref[...] = (h * _silu(z_ref[...])).astype(BF16)

    @pl.when(t == pl.num_programs(1) - 1)
    def _():
        hist_out_ref[...] = x[t_valid - (CONV_W - 1):t_valid, :]
        hlast_ref[...] = h[t_valid - 1:t_valid, :]


def _rglru(proj3, st, sl, P, l, tt, t_valid, first_pos_zero):
    b, t, _ = proj3.shape
    w = P['lru_cb'].shape[-1]
    assert t_valid >= CONV_W - 1 and (t_valid == tt or t == tt)
    col = lambda c: pl.BlockSpec((None, tt, w), lambda i, j: (i, j, c))
    per_b = lambda r: pl.BlockSpec((None, r, w), lambda i, j: (i, 0, 0))
    params = [P[k] for k in ('lru_cw', 'lru_cb', 'lru_wa', 'lru_ba', 'lru_wx', 'lru_bx', 'lru_lam')]
    return pl.pallas_call(
        functools.partial(_rglru_kernel, tt=tt, t_valid=t_valid, first_pos_zero=first_pos_zero),
        grid=(b, t // tt),
        in_specs=[col(3), col(4), _state_spec(st['conv_b'], sl), _state_spec(st['h_b'], sl)]
                 + [_layer_spec(a, l) for a in params],
        out_specs=[pl.BlockSpec((None, tt, w), lambda i, j: (i, j, 0)), per_b(CONV_W - 1), per_b(1)],
        out_shape=[SDS((b, t, w), BF16), SDS((b, CONV_W - 1, w), F32), SDS((b, 1, w), F32)],
        scratch_shapes=[pltpu.VMEM((CONV_PAD + tt, w), F32), pltpu.VMEM((1, w), F32)],
        compiler_params=_params("parallel", "arbitrary"),
        name="rglru",
    )(proj3, proj3, st['conv_b'], st['h_b'], *params)


def _later_matrix(n):
    row = lax.broadcasted_iota(jnp.int32, (n, n), 0)
    col = lax.broadcasted_iota(jnp.int32, (n, n), 1)
    return (row > col).astype(BF16)


def _sb_logits(z):
    log_b = jnp.minimum(z, 0.0) - jnp.log(1.0 + jnp.exp(-jnp.abs(z)))
    return log_b, log_b - z


def _sb_span(qb, kspan, vspan, r, later, bias, scale, mask):
    cb = later.shape[0]
    log_b, log_1mb = _sb_logits(_dot_nt(qb, kspan) * scale + bias)
    if mask is not None:
        log_1mb = jnp.where(mask, log_1mb, 0.0)
    l1b = log_1mb.astype(BF16)
    atts = []
    for j in reversed(range(kspan.shape[0] // cb)):
        js = slice(j * cb, (j + 1) * cb)
        att = jnp.exp(log_b[:, js] + _dot(l1b[:, js], later) + r)
        if mask is not None:
            att = jnp.where(mask[:, js], att, 0.0)
        atts.insert(0, att.astype(BF16))
        r = r + jnp.sum(log_1mb[:, js], axis=-1, keepdims=True)
    return _dot(jnp.concatenate(atts, axis=-1), vspan), r


def _sb_attn_kernel(bias_ref, q_ref, k_ref, v_ref, z_ref, y_ref, kb_ref, vb_ref, acc_ref, r_ref, *,
                    layer, tq, cb, scale):
    qi = pl.program_id(2)

    @pl.when(qi == 0)
    def _():
        kb_ref[...] = k_ref[...].astype(BF16)
        vb_ref[...] = v_ref[...].astype(BF16)

    bias = bias_ref[layer, pl.program_id(1)]
    later = _later_matrix(cb)
    qb = q_ref[...].astype(BF16)

    def span(k0, r, mask):
        return _sb_span(qb, kb_ref[pl.ds(k0, tq), :], vb_ref[pl.ds(k0, tq), :], r, later, bias, scale, mask)

    rr = lax.broadcasted_iota(jnp.int32, (tq, tq), 0)
    cc = lax.broadcasted_iota(jnp.int32, (tq, tq), 1)
    acc_ref[...], r_ref[...] = span(pl.multiple_of(qi * tq, tq), jnp.zeros((tq, 1), F32), cc < rr)

    def body(i, carry):
        pv, r_ref[...] = span(pl.multiple_of((qi - 1 - i) * tq, tq), r_ref[...], None)
        acc_ref[...] += pv
        return carry

    lax.fori_loop(0, qi, body, 0)
    y_ref[...] = (acc_ref[...] * _silu(z_ref[...])).astype(BF16)


def _sb_attn(proj3, bias, l, tq):
    b, t, _ = proj3.shape
    hd = LANE
    cb = min(tq, 2 * CHUNK)
    assert tq % cb == 0
    qcol, kcol, vcol, zcol = (5 * HEADS, K_GROUP * HEADS, V_GROUP * HEADS, 8 * HEADS)
    tile = lambda c0: pl.BlockSpec((None, tq, hd), lambda i, h, j: (i, j, c0 + h))
    whole = lambda c0: pl.BlockSpec((None, t, hd), lambda i, h, j: (i, 0, c0 + h))
    return pl.pallas_call(
        functools.partial(_sb_attn_kernel, layer=l, tq=tq, cb=cb, scale=hd ** -0.5),
        grid=(b, HEADS, t // tq),
        in_specs=[pl.BlockSpec(memory_space=pltpu.SMEM), tile(qcol), whole(kcol), whole(vcol), tile(zcol)],
        out_specs=pl.BlockSpec((None, tq, hd), lambda i, h, j: (i, j, h)),
        out_shape=SDS((b, t, HEADS * hd), BF16),
        scratch_shapes=[pltpu.VMEM((t, hd), BF16), pltpu.VMEM((t, hd), BF16),
                        pltpu.VMEM((tq, hd), F32), pltpu.VMEM((tq, 1), F32)],
        compiler_params=_params("parallel", "parallel", "arbitrary"),
        name="sb_attn",
    )(bias, proj3, proj3, proj3, proj3)


def _sb_decode_kernel(pt_ref, bias_ref, q_ref, kn_ref, vn_ref, z_ref, *rest, layer, n_pages, scale):
    k_pages, v_pages = rest[:n_pages], rest[n_pages:2 * n_pages]
    y_ref, acc_ref, r_ref = rest[2 * n_pages:]
    s = pl.program_id(1)
    later = _later_matrix(CHUNK)

    def update(keys, values, n_blk, masked):
        zs = []
        for h in range(HEADS):
            kh = jnp.concatenate([keys(h, g) for g in range(n_blk)], axis=0).astype(BF16)
            zh = _dot_nt(q_ref[:, h * LANE:(h + 1) * LANE].astype(BF16), kh) * scale + bias_ref[layer, h]
            zs += [zh[:, g * CHUNK:(g + 1) * CHUNK] for g in range(n_blk)]
        log_b, log_1mb = _sb_logits(jnp.concatenate(zs, axis=0))
        if masked:
            rr = lax.broadcasted_iota(jnp.int32, log_b.shape, 0) & (DEC_ROWS - 1)
            mask = lax.broadcasted_iota(jnp.int32, log_b.shape, 1) < rr
            log_1mb = jnp.where(mask, log_1mb, 0.0)
        suffix = _dot(log_1mb.astype(BF16), later)
        total = jnp.sum(log_1mb, axis=-1, keepdims=True)
        for h in range(HEADS):
            run = r_ref[h]
            atts = []
            for g in range(n_blk):
                rows = slice((h * n_blk + g) * DEC_ROWS, (h * n_blk + g + 1) * DEC_ROWS)
                att = jnp.exp(log_b[rows] + suffix[rows] + run)
                if masked:
                    att = jnp.where(mask[rows], att, 0.0)
                atts.append(att.astype(BF16))
                run = run + total[rows]
            vh = jnp.concatenate([values(h, g) for g in range(n_blk)], axis=0).astype(BF16)
            acc_ref[h] += _dot(jnp.concatenate(atts, axis=-1), vh)
            r_ref[h] = run

    @pl.when(s == 0)
    def _():
        acc_ref[...] = jnp.zeros_like(acc_ref)
        r_ref[...] = jnp.zeros_like(r_ref)
        update(lambda h, g: kn_ref[:, h * LANE:(h + 1) * LANE], lambda h, g: vn_ref[:, h * LANE:(h + 1) * LANE],
               1, True)

    update(lambda h, g: k_pages[g][pl.ds(h, CHUNK, stride=HEADS), :],
           lambda h, g: v_pages[g][pl.ds(h, CHUNK, stride=HEADS), :], n_pages, False)

    @pl.when(s == pl.num_programs(1) - 1)
    def _():
        acc = jnp.concatenate([acc_ref[h] for h in range(HEADS)], axis=-1)
        y_ref[...] = (acc * _silu(z_ref[...])).astype(BF16)


def _sb_decode(page_table, bias, q, k_new, v_new, z, cache_k, cache_v, layer, pages_per_step):
    b, n_tab = page_table.shape
    w = q.shape[-1]
    g = pages_per_step
    assert n_tab % g == 0
    page_rows, hd = cache_k.shape[2:]
    per_b = lambda r: pl.BlockSpec((None, r, w), lambda i, s, pt, bs: (i, 0, 0))

    def page_spec(j):
        return pl.BlockSpec((None, None, page_rows, hd),
                            lambda i, s, pt, bs: (layer, pt[i, n_tab - 1 - (s * g + j)], 0, 0))

    grid_spec = pltpu.PrefetchScalarGridSpec(
        num_scalar_prefetch=2,
        grid=(b, n_tab // g),
        in_specs=[per_b(DEC_ROWS), per_b(CHUNK), per_b(CHUNK), per_b(DEC_ROWS)]
                 + [page_spec(j) for j in range(g)] * 2,
        out_specs=per_b(DEC_ROWS),
        scratch_shapes=[pltpu.VMEM((HEADS, DEC_ROWS, hd), F32), pltpu.VMEM((HEADS, DEC_ROWS, hd), F32)],
    )
    return pl.pallas_call(
        functools.partial(_sb_decode_kernel, layer=layer, n_pages=g, scale=hd ** -0.5),
        grid_spec=grid_spec,
        out_shape=SDS((b, DEC_ROWS, w), BF16),
        compiler_params=_params("parallel", "arbitrary"),
        name="sb_decode",
    )(page_table, bias, q, k_new, v_new, z, *([cache_k] * g), *([cache_v] * g))


def _mlstm_kernel(x_ref, z_ref, o_ref, gc_ref, gr_ref, hist_ref, c0_ref, n0_ref, m0_ref,
                  cw_ref, cb_ref, wq_ref, wk_ref, wv_ref, brow_ref, bcol_ref, ng_ref, sk_ref,
                  y_ref, hist_out_ref, c_out_ref, n_out_ref, m_out_ref,
                  buf_ref, c_scr, n_scr, m_scr, *, t_valid, scale):
    c = pl.program_id(1)
    L = CHUNK
    h0 = CONV_PAD - (CONV_W - 1)

    @pl.when(c == 0)
    def _():
        buf_ref[h0:CONV_PAD, :] = hist_ref[...]
        c_scr[...] = c0_ref[...]
        n_scr[...] = n0_ref[...]
        m_scr[...] = m0_ref[...]

    x = x_ref[...]
    xc = _silu(_causal_conv(buf_ref, x, cw_ref, cb_ref, L))
    xcb = xc.astype(BF16)
    xb = x.astype(BF16)
    gc = gc_ref[...] + brow_ref[...]
    gr = gr_ref[...] + bcol_ref[...]
    row = lax.broadcasted_iota(jnp.int32, (L, L), 0)
    col = lax.broadcasted_iota(jnp.int32, (L, L), 1)
    causal = row >= col
    bc_all = _cumsum_dot_left(causal.astype(BF16), _log_sigmoid(gc))
    br_all = _cumsum_dot_right(_log_sigmoid(gr), (row <= col).astype(BF16))
    rowi = lax.broadcasted_iota(jnp.int32, (L, 1), 0)
    outs = []
    for h in range(HEADS):
        hs = slice(h * LANE, (h + 1) * LANE)
        q = _dot(xcb[:, hs], wq_ref[h])
        k = _dot(xcb[:, hs], wk_ref[h]) * scale
        vb = _dot(xb[:, hs], wv_ref[h]).astype(BF16)
        qb = q.astype(BF16)
        ig_c, b_c = gc[:, h:h + 1], bc_all[:, HEADS + h:HEADS + h + 1]
        ig_r, b_r = gr[h:h + 1, :], br_all[HEADS + h:HEADS + h + 1, :]
        m_h = m_scr[h:h + 1, 0:1]
        c_h = c_scr[h]
        n_h = n_scr[h:h + 1, :]
        dmat = jnp.where(causal, b_c - b_r + ig_r, -jnp.inf)
        inter = b_c + m_h
        m_t = jnp.maximum(jnp.max(dmat, axis=-1, keepdims=True), inter)
        s = _dot_nt(qb, k.astype(BF16)) * jnp.exp(dmat - m_t)
        w_inter = jnp.exp(inter - m_t)
        num = _dot(s.astype(BF16), vb) + w_inter * _dot(qb, c_h.astype(BF16))
        den = jnp.sum(s, axis=-1, keepdims=True) + w_inter * jnp.sum(q * n_h, axis=-1, keepdims=True)
        hh = num / jnp.maximum(jnp.abs(den), jnp.exp(-m_t))
        b_last = b_c[t_valid - 1:t_valid, :]
        g = b_last - b_c + ig_c
        if t_valid < L:
            g = jnp.where(rowi < t_valid, g, -jnp.inf)
        m_new = jnp.maximum(b_last + m_h, jnp.max(g, axis=0, keepdims=True))
        kw = k * jnp.exp(g - m_new)
        decay = jnp.exp(b_last + m_h - m_new)
        c_scr[h] = decay * c_h + _dot(kw.T.astype(BF16), vb)
        n_scr[h:h + 1, :] = decay * n_h + jnp.sum(kw, axis=0, keepdims=True)
        m_scr[h:h + 1, :] = jnp.broadcast_to(m_new, (1, LANE))
        hh = jax.nn.sigmoid(o_ref[:, hs]) * hh
        hc = hh - jnp.mean(hh, axis=-1, keepdims=True)
        outs.append(hc * lax.rsqrt(jnp.mean(hc * hc, axis=-1, keepdims=True) + EPS))
    hn = jnp.concatenate(outs, axis=-1) * ng_ref[...] + sk_ref[...] * xc
    y_ref[...] = (hn * _silu(z_ref[...])).astype(BF16)

    @pl.when(c == pl.num_programs(1) - 1)
    def _():
        hist_out_ref[...] = x[t_valid - (CONV_W - 1):t_valid, :]
        c_out_ref[...] = c_scr[...]
        n_out_ref[...] = n_scr[...]
        m_out_ref[...] = m_scr[...]


def _mlstm(proj3, gates_col, gates_row, st, sl, P, l, t_valid):
    b, t, _ = proj3.shape
    w = P['ml_cb'].shape[-1]
    hd = w // HEADS
    L = CHUNK
    assert t_valid >= CONV_W - 1 and (t_valid == L or t == L)
    col = lambda c: pl.BlockSpec((None, L, w), lambda i, j: (i, j, c))
    per_b = lambda *s: pl.BlockSpec((None,) + s, lambda i, j: (i,) + (0,) * len(s))
    states = [st[k] for k in ('conv_d', 'C', 'n', 'm')]
    params = [P[k] for k in ('ml_cw', 'ml_cb', 'ml_wq', 'ml_wk', 'ml_wv', 'ml_brow', 'ml_bcol', 'ml_ng', 'ml_sk')]
    return pl.pallas_call(
        functools.partial(_mlstm_kernel, t_valid=t_valid, scale=hd ** -0.5),
        grid=(b, t // L),
        in_specs=[col(9), col(10), col(11),
                  pl.BlockSpec((None, L, LANE), lambda i, j: (i, j, 0)),
                  pl.BlockSpec((None, 2 * HEADS, L), lambda i, j: (i, 0, j))]
                 + [_state_spec(a, sl) for a in states] + [_layer_spec(a, l) for a in params],
        out_specs=[pl.BlockSpec((None, L, w), lambda i, j: (i, j, 0)),
                   per_b(CONV_W - 1, w), per_b(HEADS, hd, hd), per_b(HEADS, hd), per_b(HEADS, LANE)],
        out_shape=[SDS((b, t, w), BF16), SDS((b, CONV_W - 1, w), F32), SDS((b, HEADS, hd, hd), F32),
                   SDS((b, HEADS, hd), F32), SDS((b, HEADS, LANE), F32)],
        scratch_shapes=[pltpu.VMEM((CONV_PAD + L, w), F32), pltpu.VMEM((HEADS, hd, hd), F32),
                        pltpu.VMEM((HEADS, hd), F32), pltpu.VMEM((HEADS, LANE), F32)],
        compiler_params=_params("parallel", "arbitrary"),
        name="mlstm",
    )(proj3, proj3, proj3, gates_col, gates_row, *states, *params)


def _merge_kernel(xn_ref, a_ref, b_ref, c_ref, d_ref, wg0_ref, wg1_ref, wg2_ref, wg3_ref, bg_ref, wb_ref, o_ref):
    xn = xn_ref[...]
    acc = None
    branches = (a_ref, b_ref, c_ref, d_ref)
    gates = (wg0_ref, wg1_ref, wg2_ref, wg3_ref)
    for m in range(N_BRANCH):
        gate = jax.nn.sigmoid(_dot(xn, gates[m][...]) + bg_ref[m:m + 1, :])
        term = gate * _dot(branches[m][...], wb_ref[m])
        acc = term if acc is None else acc + term
    o_ref[...] = acc.astype(BF16)


def _merge(xn, branches, P, l, tm, tn):
    rows, d = xn.shape
    w = branches[0].shape[-1]
    nj = d // tn
    gate_spec = lambda m: pl.BlockSpec((None, d, tn), lambda i, j: (l, 0, m * nj + j))
    return pl.pallas_call(
        _merge_kernel,
        grid=(rows // tm, nj),
        in_specs=[pl.BlockSpec((tm, d), lambda i, j: (i, 0))]
                 + [pl.BlockSpec((tm, w), lambda i, j: (i, 0))] * N_BRANCH
                 + [gate_spec(m) for m in range(N_BRANCH)]
                 + [pl.BlockSpec((None, N_BRANCH, tn), lambda i, j: (l, 0, j)),
                    pl.BlockSpec((None, N_BRANCH, w, tn), lambda i, j: (l, 0, 0, j))],
        out_specs=pl.BlockSpec((tm, tn), lambda i, j: (i, j)),
        out_shape=SDS((rows, d), BF16),
        compiler_params=_params("parallel", "arbitrary"),
        name="merge",
    )(xn, *branches, *([P['w_gate']] * N_BRANCH), P['b_gate'], P['w_branch'])


def _out_proj_kernel(m_ref, x_ref, w_ref, g_ref, o_ref):
    out = _dot(m_ref[...], w_ref[...])
    ms = jnp.mean(out * out, axis=-1, keepdims=True)
    o_ref[...] = x_ref[...] + out * lax.rsqrt(ms + EPS) * g_ref[...]


def _out_proj(merged, x, P, l, tm):
    rows, d = x.shape
    return pl.pallas_call(
        _out_proj_kernel,
        grid=(rows // tm,),
        in_specs=[pl.BlockSpec((tm, d), lambda i: (i, 0)),
                  pl.BlockSpec((tm, d), lambda i: (i, 0)),
                  _layer_spec(P['w_out'], l),
                  _layer_spec(P['norm_post'], l)],
        out_specs=pl.BlockSpec((tm, d), lambda i: (i, 0)),
        out_shape=SDS((rows, d), F32),
        compiler_params=_params("parallel"),
        name="out_proj",
    )(merged, x, P['w_out'], P['norm_post'])


def _tile(n, pref):
    return pref if n % pref == 0 else n


def _branches(proj3, gates_col, P, l, st, sl, t_valid, first_pos_zero, emit_vn):
    b, t, _ = proj3.shape
    gm = _gmlp(proj3, P, l, _tile(t, 4 * CHUNK), emit_vn)
    tt = _tile(t, 4 * CHUNK) if t_valid == CHUNK else t
    y_b, hist_b, h_b = _rglru(proj3, st, sl, P, l, tt, tt if t_valid == CHUNK else t_valid, first_pos_zero)
    gates_row = jnp.swapaxes(gates_col[:, :, :2 * HEADS], 1, 2)
    y_d, hist_d, c, n, m = _mlstm(proj3, gates_col, gates_row, st, sl, P, l, t_valid)
    new_st = dict(conv_b=hist_b, h_b=h_b[:, 0], conv_d=hist_d, C=c, n=n, m=m[:, :, 0])
    return gm, y_b, y_d, new_st


def _dense_tail(x, xn, branches, P, l, tm):
    merged = _merge(xn, branches, P, l, tm, _tile(x.shape[1], 256))
    return _out_proj(merged, x, P, l, _tile(x.shape[0], 256))


def _state_in(conv_b, h_b, conv_d, c, n, m):
    return dict(conv_b=conv_b, h_b=h_b[:, :, None, :], conv_d=conv_d, C=c, n=n,
                m=jnp.broadcast_to(m[..., None], m.shape + (LANE,)))


def kernel(x_prompt, x_sample, cache_k, cache_v, page_table, state_rglru_conv, state_rglru_h, state_mlstm_conv, state_mlstm_c, state_mlstm_n, state_mlstm_m, norm_pre, norm_post, w_in, gmlp_ln_g, gmlp_ln_b, gmlp_ws, gmlp_bs, lru_conv_w, lru_conv_b, lru_wa, lru_ba, lru_wx, lru_bx, lru_lambda, ml_conv_w, ml_conv_b, ml_wq, ml_wk, ml_wv, ml_bi, ml_bf, ml_norm_g, ml_skip, sb_bias, w_branch, w_gate, b_gate, w_out):
    bp, tp, d = x_prompt.shape
    bs, ts, _ = x_sample.shape
    depth = w_in.shape[0]
    w = d // N_BRANCH
    hd = w // HEADS
    n_main = 12 * w
    page = cache_k.shape[2]
    past_len = page_table.shape[1] * page
    assert tp % CHUNK == 0 and page == CHUNK and ts <= DEC_ROWS and cache_k.shape[3:] == (HEADS, hd)
    ck = cache_k.reshape(depth, cache_k.shape[1], page * HEADS, hd)
    cv = cache_v.reshape(depth, cache_v.shape[1], page * HEADS, hd)
    n_tab = page_table.shape[1]
    pages_per_step = next(g for g in (16, 8, 4, 2, 1) if n_tab % g == 0)

    row3 = lambda a: a[:, None, :]
    gate_bias = jnp.concatenate([ml_bi, ml_bf], axis=-1)
    P = dict(
        norm_pre=row3(norm_pre), norm_post=row3(norm_post),
        w_in=w_in.astype(BF16),
        w_if=jnp.pad(w_in[:, :, n_main:], ((0, 0), (0, 0), (0, LANE - 2 * HEADS))).astype(BF16),
        ln_g=row3(gmlp_ln_g), ln_b=row3(gmlp_ln_b), ws=gmlp_ws, bs_t=jnp.swapaxes(gmlp_bs, 1, 2),
        lru_cw=lru_conv_w, lru_cb=row3(lru_conv_b), lru_wa=lru_wa.astype(BF16), lru_ba=row3(lru_ba),
        lru_wx=lru_wx.astype(BF16), lru_bx=row3(lru_bx), lru_lam=row3(lru_lambda),
        ml_cw=ml_conv_w, ml_cb=row3(ml_conv_b), ml_wq=ml_wq.astype(BF16), ml_wk=ml_wk.astype(BF16),
        ml_wv=ml_wv.astype(BF16), ml_brow=row3(jnp.pad(gate_bias, ((0, 0), (0, LANE - 2 * HEADS)))),
        ml_bcol=gate_bias[:, :, None], ml_ng=row3(ml_norm_g), ml_sk=row3(ml_skip),
        w_gate=w_gate.astype(BF16), b_gate=b_gate.reshape(depth, N_BRANCH, d),
        w_branch=w_branch.astype(BF16), w_out=w_out.astype(BF16))

    xp = x_prompt.reshape(bp * tp, d)
    xs = x_sample.reshape(bs * ts, d)
    st_p = _state_in(jnp.zeros((1, bp, CONV_W - 1, w), F32), jnp.zeros((1, bp, w), F32),
                     jnp.zeros((1, bp, CONV_W - 1, w), F32), jnp.zeros((1, bp, HEADS, hd, hd), F32),
                     jnp.zeros((1, bp, HEADS, hd), F32), jnp.zeros((1, bp, HEADS), F32))
    st_s = _state_in(state_rglru_conv, state_rglru_h, state_mlstm_conv, state_mlstm_c, state_mlstm_n, state_mlstm_m)
    names = ('conv_b', 'h_b', 'conv_d', 'C', 'n', 'm')
    res_p = {k: [] for k in names}
    res_s = {k: [] for k in names}
    ks_l, vs_l, gv_l = [], [], []
    kv_p = [jnp.zeros((depth, bp * tp * HEADS, LANE), F32) for _ in range(2)]
    pad_t = lambda a, rows: jnp.pad(a, ((0, 0), (0, rows - ts), (0, 0)))

    for l in range(depth):
        proj, pif, xn, *kv_p = _in_proj(xp, P['norm_pre'], P['w_in'], P['w_if'], l, _tile(bp * tp, 1024),
                                        kv_prev=kv_p, kv_out=True)
        proj3 = proj.reshape(bp, tp, n_main)
        y_a, y_b, y_d, nst = _branches(proj3, pif.reshape(bp, tp, LANE), P, l, st_p, 0, CHUNK, True, False)
        y_c = _sb_attn(proj3, sb_bias, l, _tile(tp, 4 * CHUNK))
        xp = _dense_tail(xp, xn, [y.reshape(bp * tp, w) for y in (y_a[0], y_b, y_c, y_d)], P, l,
                         _tile(bp * tp, 1024))
        for k in names:
            res_p[k].append(nst[k])

        proj, pif, xn = _in_proj(xs, P['norm_pre'], P['w_in'], P['w_if'], l, bs * ts)
        proj3 = proj.reshape(bs, ts, n_main)
        projc = pad_t(proj3, CHUNK)
        (y_a, vn), y_b, y_d, nst = _branches(projc, pad_t(pif.reshape(bs, ts, LANE), CHUNK), P, l, st_s, l, ts,
                                             past_len == 0, True)
        y_c = _sb_decode(page_table, sb_bias, pad_t(proj3[:, :, 5 * w:6 * w], DEC_ROWS),
                         projc[:, :, K_GROUP * w:(K_GROUP + 1) * w], projc[:, :, V_GROUP * w:(V_GROUP + 1) * w],
                         pad_t(proj3[:, :, 8 * w:9 * w], DEC_ROWS), ck, cv, l, pages_per_step)
        xs = _dense_tail(xs, xn, [y[:, :ts].reshape(bs * ts, w) for y in (y_a, y_b, y_c, y_d)], P, l, bs * ts)
        for k in names:
            res_s[k].append(nst[k])
        ks_l.append(proj3[:, :, K_GROUP * w:(K_GROUP + 1) * w].reshape(bs, ts, HEADS, hd))
        vs_l.append(proj3[:, :, V_GROUP * w:(V_GROUP + 1) * w].reshape(bs, ts, HEADS, hd))
        gv_l.append(vn[:, :ts])

    st = lambda lst: jnp.stack(lst, axis=0)
    k_p, v_p = (a.reshape(depth, bp, tp, HEADS, hd) for a in kv_p)
    return (xp.reshape(bp, tp, d), xs.reshape(bs, ts, d), k_p, v_p, st(ks_l), st(vs_l),
            st(res_p['conv_b']), st(res_p['h_b']), st(res_s['conv_b']), st(res_s['h_b']),
            st(res_p['conv_d']), st(res_p['C']), st(res_p['n']), st(res_p['m']),
            st(res_s['conv_d']), st(res_s['C']), st(res_s['n']), st(res_s['m']),
            st(gv_l))
```

```python
import functools

import jax
import jax.numpy as jnp
from jax import lax
from jax.experimental import pallas as pl
from jax.experimental.pallas import tpu as pltpu

F32 = jnp.float32
BF16 = jnp.bfloat16
SDS = jax.ShapeDtypeStruct

EPS = 1e-6
N_BRANCH = 4
HEADS = 4
CONV_W = 4
CHUNK = 128
LRU_C = 8.0
LANE = 128
SUBLANE = 8
CONV_PAD = SUBLANE
DEC_ROWS = 16
K_GROUP, V_GROUP = 6, 7
VMEM_LIMIT = 56 * 1024 * 1024


def _log_sigmoid(z):
    return jnp.minimum(z, 0.0) - jnp.log1p(jnp.exp(-jnp.abs(z)))


def _silu(x):
    return x * jax.nn.sigmoid(x)


def _dot(a, b):
    return jnp.dot(a, b, preferred_element_type=F32)


def _dot_nt(a, b):
    return lax.dot_general(a, b, (((1,), (1,)), ((), ())), preferred_element_type=F32)


def _split(x):
    hi = x.astype(BF16)
    return hi, (x - hi.astype(F32)).astype(BF16)


def _cumsum_dot_right(x, m):
    hi, lo = _split(x)
    return _dot(hi, m) + _dot(lo, m)


def _cumsum_dot_left(m, x):
    hi, lo = _split(x)
    return _dot(m, hi) + _dot(m, lo)


def _params(*sem):
    return pltpu.CompilerParams(dimension_semantics=sem, vmem_limit_bytes=VMEM_LIMIT)


def _layer_spec(a, l):
    return pl.BlockSpec((None,) + a.shape[1:], lambda *_: (l,) + (0,) * (a.ndim - 1))


def _in_proj_kernel(x_ref, g_ref, w_ref, wif_ref, *rest, kv_out):
    if kv_out:
        proj_ref, pif_ref, xn_ref, k_ref, v_ref = rest[-5:]
    else:
        proj_ref, pif_ref, xn_ref = rest
    j = pl.program_id(1)

    @pl.when(j == 0)
    def _():
        x = x_ref[...]
        ms = jnp.mean(x * x, axis=-1, keepdims=True)
        xn = (x * lax.rsqrt(ms + EPS) * g_ref[...]).astype(BF16)
        xn_ref[...] = xn
        pif_ref[...] = _dot(xn, wif_ref[...])

    proj = _dot(xn_ref[...], w_ref[...])
    proj_ref[...] = proj
    if kv_out:
        tm = proj.shape[0]
        for grp, ref in ((K_GROUP, k_ref), (V_GROUP, v_ref)):
            @pl.when(j == grp)
            def _(ref=ref):
                for h in range(HEADS):
                    ref[pl.ds(h, tm, stride=HEADS), :] = proj[:, h * LANE:(h + 1) * LANE]


def _in_proj(x, g, w_in, w_if, l, tm, kv_prev=None, kv_out=False):
    rows, d = x.shape
    depth = w_in.shape[0]
    tn = d // N_BRANCH
    n = 12 * tn
    in_specs = [pl.BlockSpec((tm, d), lambda i, j: (i, 0)),
                _layer_spec(g, l),
                pl.BlockSpec((None, d, tn), lambda i, j: (l, 0, j)),
                _layer_spec(w_if, l)]
    out_specs = [pl.BlockSpec((tm, tn), lambda i, j: (i, j)),
                 pl.BlockSpec((tm, LANE), lambda i, j: (i, 0)),
                 pl.BlockSpec((tm, d), lambda i, j: (i, 0))]
    out_shape = [SDS((rows, n), F32), SDS((rows, LANE), F32), SDS((rows, d), BF16)]
    args = [x, g, w_in, w_if]
    aliases = {}
    if kv_out:
        kv_spec = pl.BlockSpec((None, tm * HEADS, LANE), lambda i, j: (l, i, 0))
        out_specs += [kv_spec, kv_spec]
        out_shape += [SDS((depth, rows * HEADS, LANE), F32)] * 2
        if kv_prev is not None:
            in_specs += [pl.BlockSpec(memory_space=pl.ANY)] * 2
            args += list(kv_prev)
            aliases = {4: 3, 5: 4}
    return pl.pallas_call(
        functools.partial(_in_proj_kernel, kv_out=kv_out),
        grid=(rows // tm, n // tn),
        in_specs=in_specs,
        out_specs=out_specs,
        out_shape=out_shape,
        input_output_aliases=aliases,
        compiler_params=_params("parallel", "arbitrary"),
        name="in_proj",
    )(*args)


def _gmlp_kernel(u_ref, v_ref, z_ref, lg_ref, lb_ref, ws_ref, bst_ref, y_ref, *vn_out, n_chunks):
    v = v_ref[...]
    vc = v - jnp.mean(v, axis=-1, keepdims=True)
    var = jnp.mean(vc * vc, axis=-1, keepdims=True)
    vn = vc * lax.rsqrt(var + EPS) * lg_ref[...] + lb_ref[...]
    if vn_out:
        vn_out[0][...] = vn
    vnb = vn.astype(BF16)
    row = lax.broadcasted_iota(jnp.int32, (CHUNK, CHUNK), 0)
    col = lax.broadcasted_iota(jnp.int32, (CHUNK, CHUNK), 1)
    for g in range(HEADS):
        gs = slice(g * LANE, (g + 1) * LANE)
        wm = jnp.where(row >= col, ws_ref[g], 0.0).astype(BF16)
        bcol = bst_ref[:, g:g + 1]
        for c in range(n_chunks):
            ts = slice(c * CHUNK, (c + 1) * CHUNK)
            s = _dot(wm, vnb[ts, gs]) + bcol
            y_ref[ts, gs] = (u_ref[ts, gs] * s * _silu(z_ref[ts, gs])).astype(BF16)


def _gmlp(proj3, P, l, tt, emit_vn):
    b, t, _ = proj3.shape
    w = P['ln_g'].shape[-1]
    col = lambda c: pl.BlockSpec((None, tt, w), lambda i, j: (i, j, c))
    out_specs = [pl.BlockSpec((None, tt, w), lambda i, j: (i, j, 0))]
    out_shape = [SDS((b, t, w), BF16)]
    if emit_vn:
        out_specs.append(pl.BlockSpec((None, tt, w), lambda i, j: (i, j, 0)))
        out_shape.append(SDS((b, t, w), F32))
    params = [P[k] for k in ('ln_g', 'ln_b', 'ws', 'bs_t')]
    return pl.pallas_call(
        functools.partial(_gmlp_kernel, n_chunks=tt // CHUNK),
        grid=(b, t // tt),
        in_specs=[col(0), col(1), col(2)] + [_layer_spec(a, l) for a in params],
        out_specs=out_specs,
        out_shape=out_shape,
        compiler_params=_params("parallel", "parallel"),
        name="gmlp",
    )(proj3, proj3, proj3, *params)


def _causal_conv(buf_ref, x, cw_ref, cb_ref, tt):
    h0 = CONV_PAD - (CONV_W - 1)
    buf_ref[CONV_PAD:CONV_PAD + tt, :] = x
    y = cb_ref[...] + cw_ref[0:1, :] * buf_ref[h0:h0 + tt, :]
    for j in range(1, CONV_W):
        y = y + cw_ref[j:j + 1, :] * buf_ref[h0 + j:h0 + j + tt, :]
    buf_ref[h0:CONV_PAD, :] = x[tt - (CONV_W - 1):tt, :]
    return y


def _state_spec(a, sl):
    return pl.BlockSpec((None, None) + a.shape[2:], lambda i, j: (sl, i) + (0,) * (a.ndim - 2))


def _rglru_kernel(x_ref, z_ref, hist_ref, h0_ref, cw_ref, cb_ref, wa_ref, ba_ref, wx_ref, bx_ref, lam_ref,
                  y_ref, hist_out_ref, hlast_ref, buf_ref, hcar_ref, *, tt, t_valid, first_pos_zero):
    t = pl.program_id(1)
    h0 = CONV_PAD - (CONV_W - 1)

    @pl.when(t == 0)
    def _():
        buf_ref[h0:CONV_PAD, :] = hist_ref[...]
        hcar_ref[...] = h0_ref[...]

    x = x_ref[...]
    xc = _causal_conv(buf_ref, x, cw_ref, cb_ref, tt)
    xcb = xc.astype(BF16)
    ra, rx = [], []
    for blk in range(HEADS):
        bs = slice(blk * LANE, (blk + 1) * LANE)
        ra.append(_dot(xcb[:, bs], wa_ref[blk]))
        rx.append(_dot(xcb[:, bs], wx_ref[blk]))
    r = jax.nn.sigmoid(jnp.concatenate(ra, axis=-1) + ba_ref[...])
    i = jax.nn.sigmoid(jnp.concatenate(rx, axis=-1) + bx_ref[...])
    log_a = LRU_C * r * _log_sigmoid(lam_ref[...])
    a = jnp.exp(log_a)
    mult = jnp.sqrt(-jnp.tanh(log_a) * (a * a + 1.0))
    rowi = lax.broadcasted_iota(jnp.int32, (tt, 1), 0)
    if first_pos_zero:
        mult = jnp.where(rowi + t * tt == 0, 1.0, mult)
    u = mult * (i * xc)
    d = 1
    while d < tt:
        if d % SUBLANE == 0:
            u = jnp.concatenate([u[:d], a[d:] * u[:tt - d] + u[d:]], axis=0)
            a = jnp.concatenate([a[:d], a[d:] * a[:tt - d]], axis=0)
        else:
            keep = rowi >= d
            u = a * jnp.where(keep, pltpu.roll(u, d, axis=0), 0.0) + u
            a = a * jnp.where(keep, pltpu.roll(a, d, axis=0), 1.0)
        d *= 2
    h = a * hcar_ref[...] + u
    hcar_ref[...] = h[tt - 1:tt, :]
    y_ref[...] = (h * _silu(z_ref[...])).astype(BF16)

    @pl.when(t == pl.num_programs(1) - 1)
    def _():
        hist_out_ref[...] = x[t_valid - (CONV_W - 1):t_valid, :]
        hlast_ref[...] = h[t_valid - 1:t_valid, :]


def _rglru(proj3, st, sl, P, l, tt, t_valid, first_pos_zero):
    b, t, _ = proj3.shape
    w = P['lru_cb'].shape[-1]
    assert t_valid >= CONV_W - 1 and (t_valid == tt or t == tt)
    col = lambda c: pl.BlockSpec((None, tt, w), lambda i, j: (i, j, c))
    per_b = lambda r: pl.BlockSpec((None, r, w), lambda i, j: (i, 0, 0))
    params = [P[k] for k in ('lru_cw', 'lru_cb', 'lru_wa', 'lru_ba', 'lru_wx', 'lru_bx', 'lru_lam')]
    return pl.pallas_call(
        functools.partial(_rglru_kernel, tt=tt, t_valid=t_valid, first_pos_zero=first_pos_zero),
        grid=(b, t // tt),
        in_specs=[col(3), col(4), _state_spec(st['conv_b'], sl), _state_spec(st['h_b'], sl)]
                 + [_layer_spec(a, l) for a in params],
        out_specs=[pl.BlockSpec((None, tt, w), lambda i, j: (i, j, 0)), per_b(CONV_W - 1), per_b(1)],
        out_shape=[SDS((b, t, w), BF16), SDS((b, CONV_W - 1, w), F32), SDS((b, 1, w), F32)],
        scratch_shapes=[pltpu.VMEM((CONV_PAD + tt, w), F32), pltpu.VMEM((1, w), F32)],
        compiler_params=_params("parallel", "arbitrary"),
        name="rglru",
    )(proj3, proj3, st['conv_b'], st['h_b'], *params)


def _later_matrix(n):
    row = lax.broadcasted_iota(jnp.int32, (n, n), 0)
    col = lax.broadcasted_iota(jnp.int32, (n, n), 1)
    return (row > col).astype(BF16)


def _sb_logits(z):
    log_b = jnp.minimum(z, 0.0) - jnp.log(1.0 + jnp.exp(-jnp.abs(z)))
    return log_b, log_b - z


def _sb_span(qb, kspan, vspan, r, later, bias, scale, mask):
    cb = later.shape[0]
    log_b, log_1mb = _sb_logits(_dot_nt(qb, kspan) * scale + bias)
    if mask is not None:
        log_1mb = jnp.where(mask, log_1mb, 0.0)
    l1b = log_1mb.astype(BF16)
    blocks = [slice(j * cb, (j + 1) * cb) for j in range(kspan.shape[0] // cb)]
    suffix = [_dot(l1b[:, js], later) for js in blocks]
    atts = []
    for js, suf in zip(reversed(blocks), reversed(suffix)):
        att = jnp.exp(log_b[:, js] + suf + r)
        if mask is not None:
            att = jnp.where(mask[:, js], att, 0.0)
        atts.insert(0, att.astype(BF16))
        r = r + jnp.sum(log_1mb[:, js], axis=-1, keepdims=True)
    return _dot(jnp.concatenate(atts, axis=-1), vspan), r


def _sb_attn_kernel(bias_ref, q_ref, k_ref, v_ref, z_ref, y_ref, kb_ref, vb_ref, acc_ref, r_ref, *,
                    layer, tq, cb, scale):
    qi = pl.program_id(2)

    @pl.when(qi == 0)
    def _():
        kb_ref[...] = k_ref[...].astype(BF16)
        vb_ref[...] = v_ref[...].astype(BF16)

    bias = bias_ref[layer, pl.program_id(1)]
    later = _later_matrix(cb)
    qb = q_ref[...].astype(BF16)

    def span(k0, n, r, mask):
        return _sb_span(qb, kb_ref[pl.ds(k0, n), :], vb_ref[pl.ds(k0, n), :], r, later, bias, scale, mask)

    rr = lax.broadcasted_iota(jnp.int32, (tq, tq), 0)
    cc = lax.broadcasted_iota(jnp.int32, (tq, tq), 1)
    acc_ref[...], r_ref[...] = span(pl.multiple_of(qi * tq, tq), tq, jnp.zeros((tq, 1), F32), cc < rr)

    def body(i, carry):
        pv, r_ref[...] = span(pl.multiple_of((qi - 2 - 2 * i) * tq, tq), 2 * tq, r_ref[...], None)
        acc_ref[...] += pv
        return carry

    lax.fori_loop(0, qi // 2, body, 0)

    @pl.when(qi % 2 == 1)
    def _():
        pv, r_ref[...] = span(0, tq, r_ref[...], None)
        acc_ref[...] += pv

    y_ref[...] = (acc_ref[...] * _silu(z_ref[...])).astype(BF16)


def _sb_attn(proj3, bias, l, tq):
    b, t, _ = proj3.shape
    hd = LANE
    cb = min(tq, 2 * CHUNK)
    assert tq % cb == 0
    qcol, kcol, vcol, zcol = (5 * HEADS, K_GROUP * HEADS, V_GROUP * HEADS, 8 * HEADS)
    tile = lambda c0: pl.BlockSpec((None, tq, hd), lambda i, h, j: (i, j, c0 + h))
    whole = lambda c0: pl.BlockSpec((None, t, hd), lambda i, h, j: (i, 0, c0 + h))
    return pl.pallas_call(
        functools.partial(_sb_attn_kernel, layer=l, tq=tq, cb=cb, scale=hd ** -0.5),
        grid=(b, HEADS, t // tq),
        in_specs=[pl.BlockSpec(memory_space=pltpu.SMEM), tile(qcol), whole(kcol), whole(vcol), tile(zcol)],
        out_specs=pl.BlockSpec((None, tq, hd), lambda i, h, j: (i, j, h)),
        out_shape=SDS((b, t, HEADS * hd), BF16),
        scratch_shapes=[pltpu.VMEM((t, hd), BF16), pltpu.VMEM((t, hd), BF16),
                        pltpu.VMEM((tq, hd), F32), pltpu.VMEM((tq, 1), F32)],
        compiler_params=_params("parallel", "parallel", "arbitrary"),
        name="sb_attn",
    )(bias, proj3, proj3, proj3, proj3)


def _sb_decode_kernel(pt_ref, bias_ref, q_ref, kn_ref, vn_ref, z_ref, *rest, layer, n_pages, scale):
    k_pages, v_pages = rest[:n_pages], rest[n_pages:2 * n_pages]
    y_ref, acc_ref, r_ref = rest[2 * n_pages:]
    s = pl.program_id(1)
    later = _later_matrix(CHUNK)

    def update(keys, values, n_blk, masked):
        zs = []
        for h in range(HEADS):
            kh = jnp.concatenate([keys(h, g) for g in range(n_blk)], axis=0).astype(BF16)
            zh = _dot_nt(q_ref[:, h * LANE:(h + 1) * LANE].astype(BF16), kh) * scale + bias_ref[layer, h]
            zs += [zh[:, g * CHUNK:(g + 1) * CHUNK] for g in range(n_blk)]
        log_b, log_1mb = _sb_logits(jnp.concatenate(zs, axis=0))
        if masked:
            rr = lax.broadcasted_iota(jnp.int32, log_b.shape, 0) & (DEC_ROWS - 1)
            mask = lax.broadcasted_iota(jnp.int32, log_b.shape, 1) < rr
            log_1mb = jnp.where(mask, log_1mb, 0.0)
        suffix = _dot(log_1mb.astype(BF16), later)
        total = jnp.sum(log_1mb, axis=-1, keepdims=True)
        for h in range(HEADS):
            run = r_ref[h]
            atts = []
            for g in range(n_blk):
                rows = slice((h * n_blk + g) * DEC_ROWS, (h * n_blk + g + 1) * DEC_ROWS)
                att = jnp.exp(log_b[rows] + suffix[rows] + run)
                if masked:
                    att = jnp.where(mask[rows], att, 0.0)
                atts.append(att.astype(BF16))
                run = run + total[rows]
            vh = jnp.concatenate([values(h, g) for g in range(n_blk)], axis=0).astype(BF16)
            acc_ref[h] += _dot(jnp.concatenate(atts, axis=-1), vh)
            r_ref[h] = run

    @pl.when(s == 0)
    def _():
        acc_ref[...] = jnp.zeros_like(acc_ref)
        r_ref[...] = jnp.zeros_like(r_ref)
        update(lambda h, g: kn_ref[:, h * LANE:(h + 1) * LANE], lambda h, g: vn_ref[:, h * LANE:(h + 1) * LANE],
               1, True)

    update(lambda h, g: k_pages[g][pl.ds(h, CHUNK, stride=HEADS), :],
           lambda h, g: v_pages[g][pl.ds(h, CHUNK, stride=HEADS), :], n_pages, False)

    @pl.when(s == pl.num_programs(1) - 1)
    def _():
        acc = jnp.concatenate([acc_ref[h] for h in range(HEADS)], axis=-1)
        y_ref[...] = (acc * _silu(z_ref[...])).astype(BF16)


def _sb_decode(page_table, bias, q, k_new, v_new, z, cache_k, cache_v, layer, pages_per_step):
    b, n_tab = page_table.shape
    w = q.shape[-1]
    g = pages_per_step
    assert n_tab % g == 0
    page_rows, hd = cache_k.shape[2:]
    per_b = lambda r: pl.BlockSpec((None, r, w), lambda i, s, pt, bs: (i, 0, 0))

    def page_spec(j):
        return pl.BlockSpec((None, None, page_rows, hd),
                            lambda i, s, pt, bs: (layer, pt[i, n_tab - 1 - (s * g + j)], 0, 0))

    grid_spec = pltpu.PrefetchScalarGridSpec(
        num_scalar_prefetch=2,
        grid=(b, n_tab // g),
        in_specs=[per_b(DEC_ROWS), per_b(CHUNK), per_b(CHUNK), per_b(DEC_ROWS)]
                 + [page_spec(j) for j in range(g)] * 2,
        out_specs=per_b(DEC_ROWS),
        scratch_shapes=[pltpu.VMEM((HEADS, DEC_ROWS, hd), F32), pltpu.VMEM((HEADS, DEC_ROWS, hd), F32)],
    )
    return pl.pallas_call(
        functools.partial(_sb_decode_kernel, layer=layer, n_pages=g, scale=hd ** -0.5),
        grid_spec=grid_spec,
        out_shape=SDS((b, DEC_ROWS, w), BF16),
        compiler_params=_params("parallel", "arbitrary"),
        name="sb_decode",
    )(page_table, bias, q, k_new, v_new, z, *([cache_k] * g), *([cache_v] * g))


def _mlstm_kernel(x_ref, z_ref, o_ref, gc_ref, gr_ref, hist_ref, c0_ref, n0_ref, m0_ref,
                  cw_ref, cb_ref, wq_ref, wk_ref, wv_ref, brow_ref, bcol_ref, ng_ref, sk_ref,
                  y_ref, hist_out_ref, c_out_ref, n_out_ref, m_out_ref,
                  buf_ref, c_scr, n_scr, m_scr, *, t_valid, scale):
    c = pl.program_id(1)
    L = CHUNK
    h0 = CONV_PAD - (CONV_W - 1)

    @pl.when(c == 0)
    def _():
        buf_ref[h0:CONV_PAD, :] = hist_ref[...]
        c_scr[...] = c0_ref[...]
        n_scr[...] = n0_ref[...]
        m_scr[...] = m0_ref[...]

    x = x_ref[...]
    xc = _silu(_causal_conv(buf_ref, x, cw_ref, cb_ref, L))
    xcb = xc.astype(BF16)
    xb = x.astype(BF16)
    gc = gc_ref[...] + brow_ref[...]
    gr = gr_ref[...] + bcol_ref[...]
    row = lax.broadcasted_iota(jnp.int32, (L, L), 0)
    col = lax.broadcasted_iota(jnp.int32, (L, L), 1)
    causal = row >= col
    bc_all = _cumsum_dot_left(causal.astype(BF16), _log_sigmoid(gc))
    br_all = _cumsum_dot_right(_log_sigmoid(gr), (row <= col).astype(BF16))
    rowi = lax.broadcasted_iota(jnp.int32, (L, 1), 0)
    H = range(HEADS)
    hs = [slice(h * LANE, (h + 1) * LANE) for h in H]
    n_all, m_all = n_scr[...], m_scr[...]
    q = [_dot(xcb[:, hs[h]], wq_ref[h]) for h in H]
    k = [_dot(xcb[:, hs[h]], wk_ref[h]) * scale for h in H]
    vb = [_dot(xb[:, hs[h]], wv_ref[h]).astype(BF16) for h in H]
    qb = [q[h].astype(BF16) for h in H]
    qk = [_dot_nt(qb[h], k[h].astype(BF16)) for h in H]
    qc = [_dot(qb[h], c_scr[h].astype(BF16)) for h in H]
    ig_c = [gc[:, h:h + 1] for h in H]
    b_c = [bc_all[:, HEADS + h:HEADS + h + 1] for h in H]
    m_h = [m_all[h:h + 1, 0:1] for h in H]
    n_h = [n_all[h:h + 1, :] for h in H]
    dmat = [jnp.where(causal, b_c[h] - br_all[HEADS + h:HEADS + h + 1, :] + gr[h:h + 1, :], -jnp.inf) for h in H]
    inter = [b_c[h] + m_h[h] for h in H]
    m_t = [jnp.maximum(jnp.max(dmat[h], axis=-1, keepdims=True), inter[h]) for h in H]
    s = [qk[h] * jnp.exp(dmat[h] - m_t[h]) for h in H]
    w_inter = [jnp.exp(inter[h] - m_t[h]) for h in H]
    num = [_dot(s[h].astype(BF16), vb[h]) + w_inter[h] * qc[h] for h in H]
    den = [jnp.sum(s[h], axis=-1, keepdims=True) + w_inter[h] * jnp.sum(q[h] * n_h[h], axis=-1, keepdims=True)
           for h in H]
    hh = [num[h] / jnp.maximum(jnp.abs(den[h]), jnp.exp(-m_t[h])) for h in H]
    b_last = [b_c[h][t_valid - 1:t_valid, :] for h in H]
    g = [b_last[h] - b_c[h] + ig_c[h] for h in H]
    if t_valid < L:
        g = [jnp.where(rowi < t_valid, g[h], -jnp.inf) for h in H]
    m_new = [jnp.maximum(b_last[h] + m_h[h], jnp.max(g[h], axis=0, keepdims=True)) for h in H]
    kw = [k[h] * jnp.exp(g[h] - m_new[h]) for h in H]
    decay = [jnp.exp(b_last[h] + m_h[h] - m_new[h]) for h in H]
    for h in H:
        c_scr[h] = decay[h] * c_scr[h] + _dot(kw[h].T.astype(BF16), vb[h])
    n_scr[...] = jnp.concatenate([decay[h] * n_h[h] + jnp.sum(kw[h], axis=0, keepdims=True) for h in H], axis=0)
    m_scr[...] = jnp.concatenate([jnp.broadcast_to(m_new[h], (1, LANE)) for h in H], axis=0)
    hh = [jax.nn.sigmoid(o_ref[:, hs[h]]) * hh[h] for h in H]
    hc = [hh[h] - jnp.mean(hh[h], axis=-1, keepdims=True) for h in H]
    outs = [hc[h] * lax.rsqrt(jnp.mean(hc[h] * hc[h], axis=-1, keepdims=True) + EPS) for h in H]
    hn = jnp.concatenate(outs, axis=-1) * ng_ref[...] + sk_ref[...] * xc
    y_ref[...] = (hn * _silu(z_ref[...])).astype(BF16)

    @pl.when(c == pl.num_programs(1) - 1)
    def _():
        hist_out_ref[...] = x[t_valid - (CONV_W - 1):t_valid, :]
        c_out_ref[...] = c_scr[...]
        n_out_ref[...] = n_scr[...]
        m_out_ref[...] = m_scr[...]


def _mlstm(proj3, gates_col, gates_row, st, sl, P, l, t_valid):
    b, t, _ = proj3.shape
    w = P['ml_cb'].shape[-1]
    hd = w // HEADS
    L = CHUNK
    assert t_valid >= CONV_W - 1 and (t_valid == L or t == L)
    col = lambda c: pl.BlockSpec((None, L, w), lambda i, j: (i, j, c))
    per_b = lambda *s: pl.BlockSpec((None,) + s, lambda i, j: (i,) + (0,) * len(s))
    states = [st[k] for k in ('conv_d', 'C', 'n', 'm')]
    params = [P[k] for k in ('ml_cw', 'ml_cb', 'ml_wq', 'ml_wk', 'ml_wv', 'ml_brow', 'ml_bcol', 'ml_ng', 'ml_sk')]
    return pl.pallas_call(
        functools.partial(_mlstm_kernel, t_valid=t_valid, scale=hd ** -0.5),
        grid=(b, t // L),
        in_specs=[col(9), col(10), col(11),
                  pl.BlockSpec((None, L, LANE), lambda i, j: (i, j, 0)),
                  pl.BlockSpec((None, 2 * HEADS, L), lambda i, j: (i, 0, j))]
                 + [_state_spec(a, sl) for a in states] + [_layer_spec(a, l) for a in params],
        out_specs=[pl.BlockSpec((None, L, w), lambda i, j: (i, j, 0)),
                   per_b(CONV_W - 1, w), per_b(HEADS, hd, hd), per_b(HEADS, hd), per_b(HEADS, LANE)],
        out_shape=[SDS((b, t, w), BF16), SDS((b, CONV_W - 1, w), F32), SDS((b, HEADS, hd, hd), F32),
                   SDS((b, HEADS, hd), F32), SDS((b, HEADS, LANE), F32)],
        scratch_shapes=[pltpu.VMEM((CONV_PAD + L, w), F32), pltpu.VMEM((HEADS, hd, hd), F32),
                        pltpu.VMEM((HEADS, hd), F32), pltpu.VMEM((HEADS, LANE), F32)],
        compiler_params=_params("parallel", "arbitrary"),
        name="mlstm",
    )(proj3, proj3, proj3, gates_col, gates_row, *states, *params)


def _merge_kernel(xn_ref, a_ref, b_ref, c_ref, d_ref, wg0_ref, wg1_ref, wg2_ref, wg3_ref, bg_ref, wb_ref, o_ref):
    xn = xn_ref[...]
    acc = None
    branches = (a_ref, b_ref, c_ref, d_ref)
    gates = (wg0_ref, wg1_ref, wg2_ref, wg3_ref)
    for m in range(N_BRANCH):
        gate = jax.nn.sigmoid(_dot(xn, gates[m][...]) + bg_ref[m:m + 1, :])
        term = gate * _dot(branches[m][...], wb_ref[m])
        acc = term if acc is None else acc + term
    o_ref[...] = acc.astype(BF16)


def _merge(xn, branches, P, l, tm, tn):
    rows, d = xn.shape
    w = branches[0].shape[-1]
    nj = d // tn
    gate_spec = lambda m: pl.BlockSpec((None, d, tn), lambda i, j: (l, 0, m * nj + j))
    return pl.pallas_call(
        _merge_kernel,
        grid=(rows // tm, nj),
        in_specs=[pl.BlockSpec((tm, d), lambda i, j: (i, 0))]
                 + [pl.BlockSpec((tm, w), lambda i, j: (i, 0))] * N_BRANCH
                 + [gate_spec(m) for m in range(N_BRANCH)]
                 + [pl.BlockSpec((None, N_BRANCH, tn), lambda i, j: (l, 0, j)),
                    pl.BlockSpec((None, N_BRANCH, w, tn), lambda i, j: (l, 0, 0, j))],
        out_specs=pl.BlockSpec((tm, tn), lambda i, j: (i, j)),
        out_shape=SDS((rows, d), BF16),
        compiler_params=_params("parallel", "arbitrary"),
        name="merge",
    )(xn, *branches, *([P['w_gate']] * N_BRANCH), P['b_gate'], P['w_branch'])


def _out_proj_kernel(m_ref, x_ref, w_ref, g_ref, o_ref):
    out = _dot(m_ref[...], w_ref[...])
    ms = jnp.mean(out * out, axis=-1, keepdims=True)
    o_ref[...] = x_ref[...] + out * lax.rsqrt(ms + EPS) * g_ref[...]


def _out_proj(merged, x, P, l, tm):
    rows, d = x.shape
    return pl.pallas_call(
        _out_proj_kernel,
        grid=(rows // tm,),
        in_specs=[pl.BlockSpec((tm, d), lambda i: (i, 0)),
                  pl.BlockSpec((tm, d), lambda i: (i, 0)),
                  _layer_spec(P['w_out'], l),
                  _layer_spec(P['norm_post'], l)],
        out_specs=pl.BlockSpec((tm, d), lambda i: (i, 0)),
        out_shape=SDS((rows, d), F32),
        compiler_params=_params("parallel"),
        name="out_proj",
    )(merged, x, P['w_out'], P['norm_post'])


def _tile(n, pref):
    return pref if n % pref == 0 else n


def _branches(proj3, gates_col, P, l, st, sl, t_valid, first_pos_zero, emit_vn):
    b, t, _ = proj3.shape
    gm = _gmlp(proj3, P, l, _tile(t, 4 * CHUNK), emit_vn)
    tt = _tile(t, 4 * CHUNK) if t_valid == CHUNK else t
    y_b, hist_b, h_b = _rglru(proj3, st, sl, P, l, tt, tt if t_valid == CHUNK else t_valid, first_pos_zero)
    gates_row = jnp.swapaxes(gates_col[:, :, :2 * HEADS], 1, 2)
    y_d, hist_d, c, n, m = _mlstm(proj3, gates_col, gates_row, st, sl, P, l, t_valid)
    new_st = dict(conv_b=hist_b, h_b=h_b[:, 0], conv_d=hist_d, C=c, n=n, m=m[:, :, 0])
    return gm, y_b, y_d, new_st


def _dense_tail(x, xn, branches, P, l, tm):
    merged = _merge(xn, branches, P, l, tm, _tile(x.shape[1], 256))
    return _out_proj(merged, x, P, l, _tile(x.shape[0], 256))


def _state_in(conv_b, h_b, conv_d, c, n, m):
    return dict(conv_b=conv_b, h_b=h_b[:, :, None, :], conv_d=conv_d, C=c, n=n,
                m=jnp.broadcast_to(m[..., None], m.shape + (LANE,)))


def kernel(x_prompt, x_sample, cache_k, cache_v, page_table, state_rglru_conv, state_rglru_h, state_mlstm_conv, state_mlstm_c, state_mlstm_n, state_mlstm_m, norm_pre, norm_post, w_in, gmlp_ln_g, gmlp_ln_b, gmlp_ws, gmlp_bs, lru_conv_w, lru_conv_b, lru_wa, lru_ba, lru_wx, lru_bx, lru_lambda, ml_conv_w, ml_conv_b, ml_wq, ml_wk, ml_wv, ml_bi, ml_bf, ml_norm_g, ml_skip, sb_bias, w_branch, w_gate, b_gate, w_out):
    bp, tp, d = x_prompt.shape
    bs, ts, _ = x_sample.shape
    depth = w_in.shape[0]
    w = d // N_BRANCH
    hd = w // HEADS
    n_main = 12 * w
    page = cache_k.shape[2]
    past_len = page_table.shape[1] * page
    assert tp % CHUNK == 0 and page == CHUNK and ts <= DEC_ROWS and cache_k.shape[3:] == (HEADS, hd)
    ck = cache_k.reshape(depth, cache_k.shape[1], page * HEADS, hd)
    cv = cache_v.reshape(depth, cache_v.shape[1], page * HEADS, hd)
    n_tab = page_table.shape[1]
    pages_per_step = next(g for g in (16, 8, 4, 2, 1) if n_tab % g == 0)

    row3 = lambda a: a[:, None, :]
    gate_bias = jnp.concatenate([ml_bi, ml_bf], axis=-1)
    P = dict(
        norm_pre=row3(norm_pre), norm_post=row3(norm_post),
        w_in=w_in.astype(BF16),
        w_if=jnp.pad(w_in[:, :, n_main:], ((0, 0), (0, 0), (0, LANE - 2 * HEADS))).astype(BF16),
        ln_g=row3(gmlp_ln_g), ln_b=row3(gmlp_ln_b), ws=gmlp_ws, bs_t=jnp.swapaxes(gmlp_bs, 1, 2),
        lru_cw=lru_conv_w, lru_cb=row3(lru_conv_b), lru_wa=lru_wa.astype(BF16), lru_ba=row3(lru_ba),
        lru_wx=lru_wx.astype(BF16), lru_bx=row3(lru_bx), lru_lam=row3(lru_lambda),
        ml_cw=ml_conv_w, ml_cb=row3(ml_conv_b), ml_wq=ml_wq.astype(BF16), ml_wk=ml_wk.astype(BF16),
        ml_wv=ml_wv.astype(BF16), ml_brow=row3(jnp.pad(gate_bias, ((0, 0), (0, LANE - 2 * HEADS)))),
        ml_bcol=gate_bias[:, :, None], ml_ng=row3(ml_norm_g), ml_sk=row3(ml_skip),
        w_gate=w_gate.astype(BF16), b_gate=b_gate.reshape(depth, N_BRANCH, d),
        w_branch=w_branch.astype(BF16), w_out=w_out.astype(BF16))

    xp = x_prompt.reshape(bp * tp, d)
    xs = x_sample.reshape(bs * ts, d)
    st_p = _state_in(jnp.zeros((1, bp, CONV_W - 1, w), F32), jnp.zeros((1, bp, w), F32),
                     jnp.zeros((1, bp, CONV_W - 1, w), F32), jnp.zeros((1, bp, HEADS, hd, hd), F32),
                     jnp.zeros((1, bp, HEADS, hd), F32), jnp.zeros((1, bp, HEADS), F32))
    st_s = _state_in(state_rglru_conv, state_rglru_h, state_mlstm_conv, state_mlstm_c, state_mlstm_n, state_mlstm_m)
    names = ('conv_b', 'h_b', 'conv_d', 'C', 'n', 'm')
    res_p = {k: [] for k in names}
    res_s = {k: [] for k in names}
    ks_l, vs_l, gv_l = [], [], []
    kv_p = [jnp.zeros((depth, bp * tp * HEADS, LANE), F32) for _ in range(2)]
    pad_t = lambda a, rows: jnp.pad(a, ((0, 0), (0, rows - ts), (0, 0)))

    for l in range(depth):
        proj, pif, xn, *kv_p = _in_proj(xp, P['norm_pre'], P['w_in'], P['w_if'], l, _tile(bp * tp, 1024),
                                        kv_prev=kv_p, kv_out=True)
        proj3 = proj.reshape(bp, tp, n_main)
        y_a, y_b, y_d, nst = _branches(proj3, pif.reshape(bp, tp, LANE), P, l, st_p, 0, CHUNK, True, False)
        y_c = _sb_attn(proj3, sb_bias, l, _tile(tp, 4 * CHUNK))
        xp = _dense_tail(xp, xn, [y.reshape(bp * tp, w) for y in (y_a[0], y_b, y_c, y_d)], P, l,
                         _tile(bp * tp, 1024))
        for k in names:
            res_p[k].append(nst[k])

        proj, pif, xn = _in_proj(xs, P['norm_pre'], P['w_in'], P['w_if'], l, bs * ts)
        proj3 = proj.reshape(bs, ts, n_main)
        projc = pad_t(proj3, CHUNK)
        (y_a, vn), y_b, y_d, nst = _branches(projc, pad_t(pif.reshape(bs, ts, LANE), CHUNK), P, l, st_s, l, ts,
                                             past_len == 0, True)
        y_c = _sb_decode(page_table, sb_bias, pad_t(proj3[:, :, 5 * w:6 * w], DEC_ROWS),
                         projc[:, :, K_GROUP * w:(K_GROUP + 1) * w], projc[:, :, V_GROUP * w:(V_GROUP + 1) * w],
                         pad_t(proj3[:, :, 8 * w:9 * w], DEC_ROWS), ck, cv, l, pages_per_step)
        xs = _dense_tail(xs, xn, [y[:, :ts].reshape(bs * ts, w) for y in (y_a, y_b, y_c, y_d)], P, l, bs * ts)
        for k in names:
            res_s[k].append(nst[k])
        ks_l.append(proj3[:, :, K_GROUP * w:(K_GROUP + 1) * w].reshape(bs, ts, HEADS, hd))
        vs_l.append(proj3[:, :, V_GROUP * w:(V_GROUP + 1) * w].reshape(bs, ts, HEADS, hd))
        gv_l.append(vn[:, :ts])

    st = lambda lst: jnp.stack(lst, axis=0)
    k_p, v_p = (a.reshape(depth, bp, tp, HEADS, hd) for a in kv_p)
    return (xp.reshape(bp, tp, d), xs.reshape(bs, ts, d), k_p, v_p, st(ks_l), st(vs_l),
            st(res_p['conv_b']), st(res_p['h_b']), st(res_s['conv_b']), st(res_s['h_b']),
            st(res_p['conv_d']), st(res_p['C']), st(res_p['n']), st(res_p['m']),
            st(res_s['conv_d']), st(res_s['C']), st(res_s['n']), st(res_s['m']),
            st(gv_l))
```

```python
import functools

import jax
import jax.numpy as jnp
from jax import lax
from jax.experimental import pallas as pl
from jax.experimental.pallas import tpu as pltpu

F32 = jnp.float32
BF16 = jnp.bfloat16
SDS = jax.ShapeDtypeStruct

EPS = 1e-6
N_BRANCH = 4
HEADS = 4
CONV_W = 4
CHUNK = 128
LRU_C = 8.0
LANE = 128
SUBLANE = 8
CONV_PAD = SUBLANE
DEC_ROWS = 16
K_GROUP, V_GROUP = 6, 7
VMEM_LIMIT = 56 * 1024 * 1024


def _log_sigmoid(z):
    return jnp.minimum(z, 0.0) - jnp.log1p(jnp.exp(-jnp.abs(z)))


def _silu(x):
    return x * jax.nn.sigmoid(x)


def _dot(a, b):
    return jnp.dot(a, b, preferred_element_type=F32)


def _dot_nt(a, b):
    return lax.dot_general(a, b, (((1,), (1,)), ((), ())), preferred_element_type=F32)


def _split(x):
    hi = x.astype(BF16)
    return hi, (x - hi.astype(F32)).astype(BF16)


def _cumsum_dot_right(x, m):
    hi, lo = _split(x)
    return _dot(hi, m) + _dot(lo, m)


def _cumsum_dot_left(m, x):
    hi, lo = _split(x)
    return _dot(m, hi) + _dot(m, lo)


def _params(*sem):
    return pltpu.CompilerParams(dimension_semantics=sem, vmem_limit_bytes=VMEM_LIMIT)


def _layer_spec(a, l):
    return pl.BlockSpec((None,) + a.shape[1:], lambda *_: (l,) + (0,) * (a.ndim - 1))


def _in_proj_kernel(x_ref, g_ref, w_ref, wif_ref, *rest, kv_out):
    if kv_out:
        proj_ref, pif_ref, xn_ref, k_ref, v_ref = rest[-5:]
    else:
        proj_ref, pif_ref, xn_ref = rest
    j = pl.program_id(1)

    @pl.when(j == 0)
    def _():
        x = x_ref[...]
        ms = jnp.mean(x * x, axis=-1, keepdims=True)
        xn = (x * lax.rsqrt(ms + EPS) * g_ref[...]).astype(BF16)
        xn_ref[...] = xn
        pif_ref[...] = _dot(xn, wif_ref[...])

    proj = _dot(xn_ref[...], w_ref[...])
    proj_ref[...] = proj
    if kv_out:
        tm = proj.shape[0]
        for grp, ref in ((K_GROUP, k_ref), (V_GROUP, v_ref)):
            @pl.when(j == grp)
            def _(ref=ref):
                for h in range(HEADS):
                    ref[pl.ds(h, tm, stride=HEADS), :] = proj[:, h * LANE:(h + 1) * LANE]


def _in_proj(x, g, w_in, w_if, l, tm, kv_prev=None, kv_out=False):
    rows, d = x.shape
    depth = w_in.shape[0]
    tn = d // N_BRANCH
    n = 12 * tn
    n_i, n_j = rows // tm, n // tn
    in_specs = [pl.BlockSpec((tm, d), lambda i, j: (jnp.minimum(i + (j >= n_j // 2), n_i - 1), 0)),
                _layer_spec(g, l),
                pl.BlockSpec((None, d, tn), lambda i, j: (l, 0, j)),
                _layer_spec(w_if, l)]
    out_specs = [pl.BlockSpec((tm, tn), lambda i, j: (i, j)),
                 pl.BlockSpec((tm, LANE), lambda i, j: (i, 0)),
                 pl.BlockSpec((tm, d), lambda i, j: (i, 0))]
    out_shape = [SDS((rows, n), F32), SDS((rows, LANE), F32), SDS((rows, d), BF16)]
    args = [x, g, w_in, w_if]
    aliases = {}
    if kv_out:
        kv_spec = pl.BlockSpec((None, tm * HEADS, LANE), lambda i, j: (l, i, 0))
        out_specs += [kv_spec, kv_spec]
        out_shape += [SDS((depth, rows * HEADS, LANE), F32)] * 2
        if kv_prev is not None:
            in_specs += [pl.BlockSpec(memory_space=pl.ANY)] * 2
            args += list(kv_prev)
            aliases = {4: 3, 5: 4}
    return pl.pallas_call(
        functools.partial(_in_proj_kernel, kv_out=kv_out),
        grid=(rows // tm, n // tn),
        in_specs=in_specs,
        out_specs=out_specs,
        out_shape=out_shape,
        input_output_aliases=aliases,
        compiler_params=_params("parallel", "arbitrary"),
        name="in_proj",
    )(*args)


def _gmlp_kernel(u_ref, v_ref, z_ref, lg_ref, lb_ref, ws_ref, bst_ref, y_ref, *vn_out, n_chunks):
    v = v_ref[...]
    vc = v - jnp.mean(v, axis=-1, keepdims=True)
    var = jnp.mean(vc * vc, axis=-1, keepdims=True)
    vn = vc * lax.rsqrt(var + EPS) * lg_ref[...] + lb_ref[...]
    if vn_out:
        vn_out[0][...] = vn
    vnb = vn.astype(BF16)
    row = lax.broadcasted_iota(jnp.int32, (CHUNK, CHUNK), 0)
    col = lax.broadcasted_iota(jnp.int32, (CHUNK, CHUNK), 1)
    for g in range(HEADS):
        gs = slice(g * LANE, (g + 1) * LANE)
        wm = jnp.where(row >= col, ws_ref[g], 0.0).astype(BF16)
        bcol = bst_ref[:, g:g + 1]
        for c in range(n_chunks):
            ts = slice(c * CHUNK, (c + 1) * CHUNK)
            s = _dot(wm, vnb[ts, gs]) + bcol
            y_ref[ts, gs] = (u_ref[ts, gs] * s * _silu(z_ref[ts, gs])).astype(BF16)


def _gmlp(proj3, P, l, tt, emit_vn):
    b, t, _ = proj3.shape
    w = P['ln_g'].shape[-1]
    col = lambda c: pl.BlockSpec((None, tt, w), lambda i, j: (i, j, c))
    out_specs = [pl.BlockSpec((None, tt, w), lambda i, j: (i, j, 0))]
    out_shape = [SDS((b, t, w), BF16)]
    if emit_vn:
        out_specs.append(pl.BlockSpec((None, tt, w), lambda i, j: (i, j, 0)))
        out_shape.append(SDS((b, t, w), F32))
    params = [P[k] for k in ('ln_g', 'ln_b', 'ws', 'bs_t')]
    return pl.pallas_call(
        functools.partial(_gmlp_kernel, n_chunks=tt // CHUNK),
        grid=(b, t // tt),
        in_specs=[col(0), col(1), col(2)] + [_layer_spec(a, l) for a in params],
        out_specs=out_specs,
        out_shape=out_shape,
        compiler_params=_params("parallel", "parallel"),
        name="gmlp",
    )(proj3, proj3, proj3, *params)


def _causal_conv(buf_ref, x, cw_ref, cb_ref, tt):
    h0 = CONV_PAD - (CONV_W - 1)
    buf_ref[CONV_PAD:CONV_PAD + tt, :] = x
    y = cb_ref[...] + cw_ref[0:1, :] * buf_ref[h0:h0 + tt, :]
    for j in range(1, CONV_W):
        y = y + cw_ref[j:j + 1, :] * buf_ref[h0 + j:h0 + j + tt, :]
    buf_ref[h0:CONV_PAD, :] = x[tt - (CONV_W - 1):tt, :]
    return y


def _state_spec(a, sl):
    return pl.BlockSpec((None, None) + a.shape[2:], lambda i, j: (sl, i) + (0,) * (a.ndim - 2))


def _rglru_kernel(x_ref, z_ref, hist_ref, h0_ref, cw_ref, cb_ref, wa_ref, ba_ref, wx_ref, bx_ref, lam_ref,
                  y_ref, hist_out_ref, hlast_ref, buf_ref, hcar_ref, *, tt, t_valid, first_pos_zero):
    t = pl.program_id(1)
    h0 = CONV_PAD - (CONV_W - 1)

    @pl.when(t == 0)
    def _():
        buf_ref[h0:CONV_PAD, :] = hist_ref[...]
        hcar_ref[...] = h0_ref[...]

    x = x_ref[...]
    xc = _causal_conv(buf_ref, x, cw_ref, cb_ref, tt)
    xcb = xc.astype(BF16)
    ra, rx = [], []
    for blk in range(HEADS):
        bs = slice(blk * LANE, (blk + 1) * LANE)
        ra.append(_dot(xcb[:, bs], wa_ref[blk]))
        rx.append(_dot(xcb[:, bs], wx_ref[blk]))
    r = jax.nn.sigmoid(jnp.concatenate(ra, axis=-1) + ba_ref[...])
    i = jax.nn.sigmoid(jnp.concatenate(rx, axis=-1) + bx_ref[...])
    log_a = LRU_C * r * _log_sigmoid(lam_ref[...])
    a = jnp.exp(log_a)
    mult = jnp.sqrt(-jnp.tanh(log_a) * (a * a + 1.0))
    rowi = lax.broadcasted_iota(jnp.int32, (tt, 1), 0)
    if first_pos_zero:
        mult = jnp.where(rowi + t * tt == 0, 1.0, mult)
    u = mult * (i * xc)
    d = 1
    while d < tt:
        if d % SUBLANE == 0:
            u = jnp.concatenate([u[:d], a[d:] * u[:tt - d] + u[d:]], axis=0)
            a = jnp.concatenate([a[:d], a[d:] * a[:tt - d]], axis=0)
        else:
            keep = rowi >= d
            u = a * jnp.where(keep, pltpu.roll(u, d, axis=0), 0.0) + u
            a = a * jnp.where(keep, pltpu.roll(a, d, axis=0), 1.0)
        d *= 2
    h = a * hcar_ref[...] + u
    hcar_ref[...] = h[tt - 1:tt, :]
    y_ref[...] = (h * _silu(z_ref[...])).astype(BF16)

    @pl.when(t == pl.num_programs(1) - 1)
    def _():
        hist_out_ref[...] = x[t_valid - (CONV_W - 1):t_valid, :]
        hlast_ref[...] = h[t_valid - 1:t_valid, :]


def _rglru(proj3, st, sl, P, l, tt, t_valid, first_pos_zero):
    b, t, _ = proj3.shape
    w = P['lru_cb'].shape[-1]
    assert t_valid >= CONV_W - 1 and (t_valid == tt or t == tt)
    col = lambda c: pl.BlockSpec((None, tt, w), lambda i, j: (i, j, c))
    per_b = lambda r: pl.BlockSpec((None, r, w), lambda i, j: (i, 0, 0))
    params = [P[k] for k in ('lru_cw', 'lru_cb', 'lru_wa', 'lru_ba', 'lru_wx', 'lru_bx', 'lru_lam')]
    return pl.pallas_call(
        functools.partial(_rglru_kernel, tt=tt, t_valid=t_valid, first_pos_zero=first_pos_zero),
        grid=(b, t // tt),
        in_specs=[col(3), col(4), _state_spec(st['conv_b'], sl), _state_spec(st['h_b'], sl)]
                 + [_layer_spec(a, l) for a in params],
        out_specs=[pl.BlockSpec((None, tt, w), lambda i, j: (i, j, 0)), per_b(CONV_W - 1), per_b(1)],
        out_shape=[SDS((b, t, w), BF16), SDS((b, CONV_W - 1, w), F32), SDS((b, 1, w), F32)],
        scratch_shapes=[pltpu.VMEM((CONV_PAD + tt, w), F32), pltpu.VMEM((1, w), F32)],
        compiler_params=_params("parallel", "arbitrary"),
        name="rglru",
    )(proj3, proj3, st['conv_b'], st['h_b'], *params)


def _later_matrix(n):
    row = lax.broadcasted_iota(jnp.int32, (n, n), 0)
    col = lax.broadcasted_iota(jnp.int32, (n, n), 1)
    return (row > col).astype(BF16)


def _sb_logits(z):
    log_b = jnp.minimum(z, 0.0) - jnp.log(1.0 + jnp.exp(-jnp.abs(z)))
    return log_b, log_b - z


def _sb_span(qb, kspan, vspan, r, later, bias, scale, mask):
    cb = later.shape[0]
    log_b, log_1mb = _sb_logits(_dot_nt(qb, kspan) * scale + bias)
    if mask is not None:
        log_1mb = jnp.where(mask, log_1mb, 0.0)
    l1b = log_1mb.astype(BF16)
    blocks = [slice(j * cb, (j + 1) * cb) for j in range(kspan.shape[0] // cb)]
    suffix = [_dot(l1b[:, js], later) for js in blocks]
    atts = []
    for js, suf in zip(reversed(blocks), reversed(suffix)):
        att = jnp.exp(log_b[:, js] + suf + r)
        if mask is not None:
            att = jnp.where(mask[:, js], att, 0.0)
        atts.insert(0, att.astype(BF16))
        r = r + jnp.sum(log_1mb[:, js], axis=-1, keepdims=True)
    return _dot(jnp.concatenate(atts, axis=-1), vspan), r


def _sb_attn_kernel(bias_ref, q_ref, k_ref, v_ref, z_ref, y_ref, kb_ref, vb_ref, acc_ref, r_ref, *,
                    layer, tq, cb, scale):
    qi = pl.program_id(2)

    @pl.when(qi == 0)
    def _():
        kb_ref[...] = k_ref[...].astype(BF16)
        vb_ref[...] = v_ref[...].astype(BF16)

    bias = bias_ref[layer, pl.program_id(1)]
    later = _later_matrix(cb)
    qb = q_ref[...].astype(BF16)

    def span(k0, n, r, mask):
        return _sb_span(qb, kb_ref[pl.ds(k0, n), :], vb_ref[pl.ds(k0, n), :], r, later, bias, scale, mask)

    rr = lax.broadcasted_iota(jnp.int32, (tq, tq), 0)
    cc = lax.broadcasted_iota(jnp.int32, (tq, tq), 1)
    acc_ref[...], r_ref[...] = span(pl.multiple_of(qi * tq, tq), tq, jnp.zeros((tq, 1), F32), cc < rr)

    def body(i, carry):
        pv, r_ref[...] = span(pl.multiple_of((qi - 2 - 2 * i) * tq, tq), 2 * tq, r_ref[...], None)
        acc_ref[...] += pv
        return carry

    lax.fori_loop(0, qi // 2, body, 0)

    @pl.when(qi % 2 == 1)
    def _():
        pv, r_ref[...] = span(0, tq, r_ref[...], None)
        acc_ref[...] += pv

    y_ref[...] = (acc_ref[...] * _silu(z_ref[...])).astype(BF16)


def _sb_attn(proj3, bias, l, tq):
    b, t, _ = proj3.shape
    hd = LANE
    cb = min(tq, 2 * CHUNK)
    assert tq % cb == 0
    qcol, kcol, vcol, zcol = (5 * HEADS, K_GROUP * HEADS, V_GROUP * HEADS, 8 * HEADS)
    tile = lambda c0: pl.BlockSpec((None, tq, hd), lambda i, h, j: (i, j, c0 + h))
    whole = lambda c0: pl.BlockSpec((None, t, hd), lambda i, h, j: (i, 0, c0 + h))
    return pl.pallas_call(
        functools.partial(_sb_attn_kernel, layer=l, tq=tq, cb=cb, scale=hd ** -0.5),
        grid=(b, HEADS, t // tq),
        in_specs=[pl.BlockSpec(memory_space=pltpu.SMEM), tile(qcol), whole(kcol), whole(vcol), tile(zcol)],
        out_specs=pl.BlockSpec((None, tq, hd), lambda i, h, j: (i, j, h)),
        out_shape=SDS((b, t, HEADS * hd), BF16),
        scratch_shapes=[pltpu.VMEM((t, hd), BF16), pltpu.VMEM((t, hd), BF16),
                        pltpu.VMEM((tq, hd), F32), pltpu.VMEM((tq, 1), F32)],
        compiler_params=_params("parallel", "parallel", "arbitrary"),
        name="sb_attn",
    )(bias, proj3, proj3, proj3, proj3)


def _sb_decode_kernel(pt_ref, bias_ref, q_ref, kn_ref, vn_ref, z_ref, *rest, layer, n_pages, scale):
    k_pages, v_pages = rest[:n_pages], rest[n_pages:2 * n_pages]
    y_ref, acc_ref, r_ref = rest[2 * n_pages:]
    s = pl.program_id(1)
    later = _later_matrix(CHUNK)

    def update(keys, values, n_blk, masked):
        zs = []
        for h in range(HEADS):
            kh = jnp.concatenate([keys(h, g) for g in range(n_blk)], axis=0).astype(BF16)
            zh = _dot_nt(q_ref[:, h * LANE:(h + 1) * LANE].astype(BF16), kh) * scale + bias_ref[layer, h]
            zs += [zh[:, g * CHUNK:(g + 1) * CHUNK] for g in range(n_blk)]
        log_b, log_1mb = _sb_logits(jnp.concatenate(zs, axis=0))
        if masked:
            rr = lax.broadcasted_iota(jnp.int32, log_b.shape, 0) & (DEC_ROWS - 1)
            mask = lax.broadcasted_iota(jnp.int32, log_b.shape, 1) < rr
            log_1mb = jnp.where(mask, log_1mb, 0.0)
        suffix = _dot(log_1mb.astype(BF16), later)
        total = jnp.sum(log_1mb, axis=-1, keepdims=True)
        for h in range(HEADS):
            run = r_ref[h]
            atts = []
            for g in range(n_blk):
                rows = slice((h * n_blk + g) * DEC_ROWS, (h * n_blk + g + 1) * DEC_ROWS)
                att = jnp.exp(log_b[rows] + suffix[rows] + run)
                if masked:
                    att = jnp.where(mask[rows], att, 0.0)
                atts.append(att.astype(BF16))
                run = run + total[rows]
            vh = jnp.concatenate([values(h, g) for g in range(n_blk)], axis=0).astype(BF16)
            acc_ref[h] += _dot(jnp.concatenate(atts, axis=-1), vh)
            r_ref[h] = run

    @pl.when(s == 0)
    def _():
        acc_ref[...] = jnp.zeros_like(acc_ref)
        r_ref[...] = jnp.zeros_like(r_ref)
        update(lambda h, g: kn_ref[:, h * LANE:(h + 1) * LANE], lambda h, g: vn_ref[:, h * LANE:(h + 1) * LANE],
               1, True)

    update(lambda h, g: k_pages[g][pl.ds(h, CHUNK, stride=HEADS), :],
           lambda h, g: v_pages[g][pl.ds(h, CHUNK, stride=HEADS), :], n_pages, False)

    @pl.when(s == pl.num_programs(1) - 1)
    def _():
        acc = jnp.concatenate([acc_ref[h] for h in range(HEADS)], axis=-1)
        y_ref[...] = (acc * _silu(z_ref[...])).astype(BF16)


def _sb_decode(page_table, bias, q, k_new, v_new, z, cache_k, cache_v, layer, pages_per_step):
    b, n_tab = page_table.shape
    w = q.shape[-1]
    g = pages_per_step
    assert n_tab % g == 0
    page_rows, hd = cache_k.shape[2:]
    per_b = lambda r: pl.BlockSpec((None, r, w), lambda i, s, pt, bs: (i, 0, 0))

    def page_spec(j):
        return pl.BlockSpec((None, None, page_rows, hd),
                            lambda i, s, pt, bs: (layer, pt[i, n_tab - 1 - (s * g + j)], 0, 0))

    grid_spec = pltpu.PrefetchScalarGridSpec(
        num_scalar_prefetch=2,
        grid=(b, n_tab // g),
        in_specs=[per_b(DEC_ROWS), per_b(CHUNK), per_b(CHUNK), per_b(DEC_ROWS)]
                 + [page_spec(j) for j in range(g)] * 2,
        out_specs=per_b(DEC_ROWS),
        scratch_shapes=[pltpu.VMEM((HEADS, DEC_ROWS, hd), F32), pltpu.VMEM((HEADS, DEC_ROWS, hd), F32)],
    )
    return pl.pallas_call(
        functools.partial(_sb_decode_kernel, layer=layer, n_pages=g, scale=hd ** -0.5),
        grid_spec=grid_spec,
        out_shape=SDS((b, DEC_ROWS, w), BF16),
        compiler_params=_params("parallel", "arbitrary"),
        name="sb_decode",
    )(page_table, bias, q, k_new, v_new, z, *([cache_k] * g), *([cache_v] * g))


def _mlstm_kernel(x_ref, z_ref, o_ref, gc_ref, gr_ref, hist_ref, c0_ref, n0_ref, m0_ref,
                  cw_ref, cb_ref, wq_ref, wk_ref, wv_ref, brow_ref, bcol_ref, ng_ref, sk_ref,
                  y_ref, hist_out_ref, c_out_ref, n_out_ref, m_out_ref,
                  buf_ref, c_scr, n_scr, m_scr, *, t_valid, scale):
    c = pl.program_id(1)
    L = CHUNK
    h0 = CONV_PAD - (CONV_W - 1)

    @pl.when(c == 0)
    def _():
        buf_ref[h0:CONV_PAD, :] = hist_ref[...]
        c_scr[...] = c0_ref[...]
        n_scr[...] = n0_ref[...]
        m_scr[...] = m0_ref[...]

    x = x_ref[...]
    xc = _silu(_causal_conv(buf_ref, x, cw_ref, cb_ref, L))
    xcb = xc.astype(BF16)
    xb = x.astype(BF16)
    gc = gc_ref[...] + brow_ref[...]
    gr = gr_ref[...] + bcol_ref[...]
    row = lax.broadcasted_iota(jnp.int32, (L, L), 0)
    col = lax.broadcasted_iota(jnp.int32, (L, L), 1)
    causal = row >= col
    bc_all = _cumsum_dot_left(causal.astype(BF16), _log_sigmoid(gc))
    br_all = _cumsum_dot_right(_log_sigmoid(gr), (row <= col).astype(BF16))
    rowi = lax.broadcasted_iota(jnp.int32, (L, 1), 0)
    H = range(HEADS)
    hs = [slice(h * LANE, (h + 1) * LANE) for h in H]
    n_all, m_all = n_scr[...], m_scr[...]
    q = [_dot(xcb[:, hs[h]], wq_ref[h]) for h in H]
    k = [_dot(xcb[:, hs[h]], wk_ref[h]) * scale for h in H]
    vb = [_dot(xb[:, hs[h]], wv_ref[h]).astype(BF16) for h in H]
    qb = [q[h].astype(BF16) for h in H]
    qk = [_dot_nt(qb[h], k[h].astype(BF16)) for h in H]
    qc = [_dot(qb[h], c_scr[h].astype(BF16)) for h in H]
    ig_c = [gc[:, h:h + 1] for h in H]
    b_c = [bc_all[:, HEADS + h:HEADS + h + 1] for h in H]
    m_h = [m_all[h:h + 1, 0:1] for h in H]
    n_h = [n_all[h:h + 1, :] for h in H]
    dmat = [jnp.where(causal, b_c[h] - br_all[HEADS + h:HEADS + h + 1, :] + gr[h:h + 1, :], -jnp.inf) for h in H]
    inter = [b_c[h] + m_h[h] for h in H]
    m_t = [jnp.maximum(jnp.max(dmat[h], axis=-1, keepdims=True), inter[h]) for h in H]
    s = [qk[h] * jnp.exp(dmat[h] - m_t[h]) for h in H]
    w_inter = [jnp.exp(inter[h] - m_t[h]) for h in H]
    num = [_dot(s[h].astype(BF16), vb[h]) + w_inter[h] * qc[h] for h in H]
    den = [jnp.sum(s[h], axis=-1, keepdims=True) + w_inter[h] * jnp.sum(q[h] * n_h[h], axis=-1, keepdims=True)
           for h in H]
    hh = [num[h] / jnp.maximum(jnp.abs(den[h]), jnp.exp(-m_t[h])) for h in H]
    b_last = [b_c[h][t_valid - 1:t_valid, :] for h in H]
    g = [b_last[h] - b_c[h] + ig_c[h] for h in H]
    if t_valid < L:
        g = [jnp.where(rowi < t_valid, g[h], -jnp.inf) for h in H]
    m_new = [jnp.maximum(b_last[h] + m_h[h], jnp.max(g[h], axis=0, keepdims=True)) for h in H]
    kw = [k[h] * jnp.exp(g[h] - m_new[h]) for h in H]
    decay = [jnp.exp(b_last[h] + m_h[h] - m_new[h]) for h in H]
    for h in H:
        c_scr[h] = decay[h] * c_scr[h] + _dot(kw[h].T.astype(BF16), vb[h])
    n_scr[...] = jnp.concatenate([decay[h] * n_h[h] + jnp.sum(kw[h], axis=0, keepdims=True) for h in H], axis=0)
    m_scr[...] = jnp.concatenate([jnp.broadcast_to(m_new[h], (1, LANE)) for h in H], axis=0)
    hh = [jax.nn.sigmoid(o_ref[:, hs[h]]) * hh[h] for h in H]
    hc = [hh[h] - jnp.mean(hh[h], axis=-1, keepdims=True) for h in H]
    outs = [hc[h] * lax.rsqrt(jnp.mean(hc[h] * hc[h], axis=-1, keepdims=True) + EPS) for h in H]
    hn = jnp.concatenate(outs, axis=-1) * ng_ref[...] + sk_ref[...] * xc
    y_ref[...] = (hn * _silu(z_ref[...])).astype(BF16)

    @pl.when(c == pl.num_programs(1) - 1)
    def _():
        hist_out_ref[...] = x[t_valid - (CONV_W - 1):t_valid, :]
        c_out_ref[...] = c_scr[...]
        n_out_ref[...] = n_scr[...]
        m_out_ref[...] = m_scr[...]


def _mlstm(proj3, gates_col, gates_row, st, sl, P, l, t_valid):
    b, t, _ = proj3.shape
    w = P['ml_cb'].shape[-1]
    hd = w // HEADS
    L = CHUNK
    assert t_valid >= CONV_W - 1 and (t_valid == L or t == L)
    col = lambda c: pl.BlockSpec((None, L, w), lambda i, j: (i, j, c))
    per_b = lambda *s: pl.BlockSpec((None,) + s, lambda i, j: (i,) + (0,) * len(s))
    states = [st[k] for k in ('conv_d', 'C', 'n', 'm')]
    params = [P[k] for k in ('ml_cw', 'ml_cb', 'ml_wq', 'ml_wk', 'ml_wv', 'ml_brow', 'ml_bcol', 'ml_ng', 'ml_sk')]
    return pl.pallas_call(
        functools.partial(_mlstm_kernel, t_valid=t_valid, scale=hd ** -0.5),
        grid=(b, t // L),
        in_specs=[col(9), col(10), col(11),
                  pl.BlockSpec((None, L, LANE), lambda i, j: (i, j, 0)),
                  pl.BlockSpec((None, 2 * HEADS, L), lambda i, j: (i, 0, j))]
                 + [_state_spec(a, sl) for a in states] + [_layer_spec(a, l) for a in params],
        out_specs=[pl.BlockSpec((None, L, w), lambda i, j: (i, j, 0)),
                   per_b(CONV_W - 1, w), per_b(HEADS, hd, hd), per_b(HEADS, hd), per_b(HEADS, LANE)],
        out_shape=[SDS((b, t, w), BF16), SDS((b, CONV_W - 1, w), F32), SDS((b, HEADS, hd, hd), F32),
                   SDS((b, HEADS, hd), F32), SDS((b, HEADS, LANE), F32)],
        scratch_shapes=[pltpu.VMEM((CONV_PAD + L, w), F32), pltpu.VMEM((HEADS, hd, hd), F32),
                        pltpu.VMEM((HEADS, hd), F32), pltpu.VMEM((HEADS, LANE), F32)],
        compiler_params=_params("parallel", "arbitrary"),
        name="mlstm",
    )(proj3, proj3, proj3, gates_col, gates_row, *states, *params)


def _merge_kernel(xn_ref, a_ref, b_ref, c_ref, d_ref, wg0_ref, wg1_ref, wg2_ref, wg3_ref, bg_ref, wb_ref, o_ref):
    xn = xn_ref[...]
    acc = None
    branches = (a_ref, b_ref, c_ref, d_ref)
    gates = (wg0_ref, wg1_ref, wg2_ref, wg3_ref)
    for m in range(N_BRANCH):
        gate = jax.nn.sigmoid(_dot(xn, gates[m][...]) + bg_ref[m:m + 1, :])
        term = gate * _dot(branches[m][...], wb_ref[m])
        acc = term if acc is None else acc + term
    o_ref[...] = acc.astype(BF16)


def _merge(xn, branches, P, l, tm, tn):
    rows, d = xn.shape
    w = branches[0].shape[-1]
    nj = d // tn
    gate_spec = lambda m: pl.BlockSpec((None, d, tn), lambda i, j: (l, 0, m * nj + j))
    return pl.pallas_call(
        _merge_kernel,
        grid=(rows // tm, nj),
        in_specs=[pl.BlockSpec((tm, d), lambda i, j: (i, 0))]
                 + [pl.BlockSpec((tm, w), lambda i, j: (i, 0))] * N_BRANCH
                 + [gate_spec(m) for m in range(N_BRANCH)]
                 + [pl.BlockSpec((None, N_BRANCH, tn), lambda i, j: (l, 0, j)),
                    pl.BlockSpec((None, N_BRANCH, w, tn), lambda i, j: (l, 0, 0, j))],
        out_specs=pl.BlockSpec((tm, tn), lambda i, j: (i, j)),
        out_shape=SDS((rows, d), BF16),
        compiler_params=_params("parallel", "arbitrary"),
        name="merge",
    )(xn, *branches, *([P['w_gate']] * N_BRANCH), P['b_gate'], P['w_branch'])


def _out_proj_kernel(m_ref, x_ref, w_ref, g_ref, o_ref):
    out = _dot(m_ref[...], w_ref[...])
    ms = jnp.mean(out * out, axis=-1, keepdims=True)
    o_ref[...] = x_ref[...] + out * lax.rsqrt(ms + EPS) * g_ref[...]


def _out_proj(merged, x, P, l, tm):
    rows, d = x.shape
    return pl.pallas_call(
        _out_proj_kernel,
        grid=(rows // tm,),
        in_specs=[pl.BlockSpec((tm, d), lambda i: (i, 0)),
                  pl.BlockSpec((tm, d), lambda i: (i, 0)),
                  _layer_spec(P['w_out'], l),
                  _layer_spec(P['norm_post'], l)],
        out_specs=pl.BlockSpec((tm, d), lambda i: (i, 0)),
        out_shape=SDS((rows, d), F32),
        compiler_params=_params("parallel"),
        name="out_proj",
    )(merged, x, P['w_out'], P['norm_post'])


def _tile(n, pref):
    return pref if n % pref == 0 else n


def _branches(proj3, gates_col, P, l, st, sl, t_valid, first_pos_zero, emit_vn):
    b, t, _ = proj3.shape
    gm = _gmlp(proj3, P, l, _tile(t, 4 * CHUNK), emit_vn)
    tt = _tile(t, 4 * CHUNK) if t_valid == CHUNK else t
    y_b, hist_b, h_b = _rglru(proj3, st, sl, P, l, tt, tt if t_valid == CHUNK else t_valid, first_pos_zero)
    gates_row = jnp.swapaxes(gates_col[:, :, :2 * HEADS], 1, 2)
    y_d, hist_d, c, n, m = _mlstm(proj3, gates_col, gates_row, st, sl, P, l, t_valid)
    new_st = dict(conv_b=hist_b, h_b=h_b[:, 0], conv_d=hist_d, C=c, n=n, m=m[:, :, 0])
    return gm, y_b, y_d, new_st


def _dense_tail(x, xn, branches, P, l, tm):
    merged = _merge(xn, branches, P, l, tm, _tile(x.shape[1], 512))
    return _out_proj(merged, x, P, l, _tile(x.shape[0], 256))


def _state_in(conv_b, h_b, conv_d, c, n, m):
    return dict(conv_b=conv_b, h_b=h_b[:, :, None, :], conv_d=conv_d, C=c, n=n,
                m=jnp.broadcast_to(m[..., None], m.shape + (LANE,)))


def kernel(x_prompt, x_sample, cache_k, cache_v, page_table, state_rglru_conv, state_rglru_h, state_mlstm_conv, state_mlstm_c, state_mlstm_n, state_mlstm_m, norm_pre, norm_post, w_in, gmlp_ln_g, gmlp_ln_b, gmlp_ws, gmlp_bs, lru_conv_w, lru_conv_b, lru_wa, lru_ba, lru_wx, lru_bx, lru_lambda, ml_conv_w, ml_conv_b, ml_wq, ml_wk, ml_wv, ml_bi, ml_bf, ml_norm_g, ml_skip, sb_bias, w_branch, w_gate, b_gate, w_out):
    bp, tp, d = x_prompt.shape
    bs, ts, _ = x_sample.shape
    depth = w_in.shape[0]
    w = d // N_BRANCH
    hd = w // HEADS
    n_main = 12 * w
    page = cache_k.shape[2]
    past_len = page_table.shape[1] * page
    assert tp % CHUNK == 0 and page == CHUNK and ts <= DEC_ROWS and cache_k.shape[3:] == (HEADS, hd)
    ck = cache_k.reshape(depth, cache_k.shape[1], page * HEADS, hd)
    cv = cache_v.reshape(depth, cache_v.shape[1], page * HEADS, hd)
    n_tab = page_table.shape[1]
    pages_per_step = next(g for g in (16, 8, 4, 2, 1) if n_tab % g == 0)

    row3 = lambda a: a[:, None, :]
    gate_bias = jnp.concatenate([ml_bi, ml_bf], axis=-1)
    P = dict(
        norm_pre=row3(norm_pre), norm_post=row3(norm_post),
        w_in=w_in[:, :, :n_main].astype(BF16),
        w_if=jnp.pad(w_in[:, :, n_main:], ((0, 0), (0, 0), (0, LANE - 2 * HEADS))).astype(BF16),
        ln_g=row3(gmlp_ln_g), ln_b=row3(gmlp_ln_b), ws=gmlp_ws, bs_t=jnp.swapaxes(gmlp_bs, 1, 2),
        lru_cw=lru_conv_w, lru_cb=row3(lru_conv_b), lru_wa=lru_wa.astype(BF16), lru_ba=row3(lru_ba),
        lru_wx=lru_wx.astype(BF16), lru_bx=row3(lru_bx), lru_lam=row3(lru_lambda),
        ml_cw=ml_conv_w, ml_cb=row3(ml_conv_b), ml_wq=ml_wq.astype(BF16), ml_wk=ml_wk.astype(BF16),
        ml_wv=ml_wv.astype(BF16), ml_brow=row3(jnp.pad(gate_bias, ((0, 0), (0, LANE - 2 * HEADS)))),
        ml_bcol=gate_bias[:, :, None], ml_ng=row3(ml_norm_g), ml_sk=row3(ml_skip),
        w_gate=w_gate.astype(BF16), b_gate=b_gate.reshape(depth, N_BRANCH, d),
        w_branch=w_branch.astype(BF16), w_out=w_out.astype(BF16))

    xp = x_prompt.reshape(bp * tp, d)
    xs = x_sample.reshape(bs * ts, d)
    st_p = _state_in(jnp.zeros((1, bp, CONV_W - 1, w), F32), jnp.zeros((1, bp, w), F32),
                     jnp.zeros((1, bp, CONV_W - 1, w), F32), jnp.zeros((1, bp, HEADS, hd, hd), F32),
                     jnp.zeros((1, bp, HEADS, hd), F32), jnp.zeros((1, bp, HEADS), F32))
    st_s = _state_in(state_rglru_conv, state_rglru_h, state_mlstm_conv, state_mlstm_c, state_mlstm_n, state_mlstm_m)
    names = ('conv_b', 'h_b', 'conv_d', 'C', 'n', 'm')
    res_p = {k: [] for k in names}
    res_s = {k: [] for k in names}
    ks_l, vs_l, gv_l = [], [], []
    kv_p = [jnp.zeros((depth, bp * tp * HEADS, LANE), F32) for _ in range(2)]
    pad_t = lambda a, rows: jnp.pad(a, ((0, 0), (0, rows - ts), (0, 0)))

    for l in range(depth):
        proj, pif, xn, *kv_p = _in_proj(xp, P['norm_pre'], P['w_in'], P['w_if'], l, _tile(bp * tp, 1024),
                                        kv_prev=kv_p, kv_out=True)
        proj3 = proj.reshape(bp, tp, n_main)
        y_a, y_b, y_d, nst = _branches(proj3, pif.reshape(bp, tp, LANE), P, l, st_p, 0, CHUNK, True, False)
        y_c = _sb_attn(proj3, sb_bias, l, _tile(tp, 4 * CHUNK))
        xp = _dense_tail(xp, xn, [y.reshape(bp * tp, w) for y in (y_a[0], y_b, y_c, y_d)], P, l,
                         _tile(bp * tp, 1024))
        for k in names:
            res_p[k].append(nst[k])

        proj, pif, xn = _in_proj(xs, P['norm_pre'], P['w_in'], P['w_if'], l, bs * ts)
        proj3 = proj.reshape(bs, ts, n_main)
        projc = pad_t(proj3, CHUNK)
        (y_a, vn), y_b, y_d, nst = _branches(projc, pad_t(pif.reshape(bs, ts, LANE), CHUNK), P, l, st_s, l, ts,
                                             past_len == 0, True)
        y_c = _sb_decode(page_table, sb_bias, pad_t(proj3[:, :, 5 * w:6 * w], DEC_ROWS),
                         projc[:, :, K_GROUP * w:(K_GROUP + 1) * w], projc[:, :, V_GROUP * w:(V_GROUP + 1) * w],
                         pad_t(proj3[:, :, 8 * w:9 * w], DEC_ROWS), ck, cv, l, pages_per_step)
        xs = _dense_tail(xs, xn, [y[:, :ts].reshape(bs * ts, w) for y in (y_a, y_b, y_c, y_d)], P, l, bs * ts)
        for k in names:
            res_s[k].append(nst[k])
        ks_l.append(proj3[:, :, K_GROUP * w:(K_GROUP + 1) * w].reshape(bs, ts, HEADS, hd))
        vs_l.append(proj3[:, :, V_GROUP * w:(V_GROUP + 1) * w].reshape(bs, ts, HEADS, hd))
        gv_l.append(vn[:, :ts])

    st = lambda lst: jnp.stack(lst, axis=0)
    k_p, v_p = (a.reshape(depth, bp, tp, HEADS, hd) for a in kv_p)
    return (xp.reshape(bp, tp, d), xs.reshape(bs, ts, d), k_p, v_p, st(ks_l), st(vs_l),
            st(res_p['conv_b']), st(res_p['h_b']), st(res_s['conv_b']), st(res_s['h_b']),
            st(res_p['conv_d']), st(res_p['C']), st(res_p['n']), st(res_p['m']),
            st(res_s['conv_d']), st(res_s['C']), st(res_s['n']), st(res_s['m']),
            st(gv_l))
```

```python
import functools

import jax
import jax.numpy as jnp
from jax import lax
from jax.experimental import pallas as pl
from jax.experimental.pallas import tpu as pltpu

F32 = jnp.float32
BF16 = jnp.bfloat16
SDS = jax.ShapeDtypeStruct

EPS = 1e-6
N_BRANCH = 4
HEADS = 4
CONV_W = 4
CHUNK = 128
LRU_C = 8.0
LANE = 128
SUBLANE = 8
CONV_PAD = SUBLANE
DEC_ROWS = 16
K_GROUP, V_GROUP = 6, 7
VMEM_LIMIT = 56 * 1024 * 1024


def _log_sigmoid(z):
    return jnp.minimum(z, 0.0) - jnp.log1p(jnp.exp(-jnp.abs(z)))


def _silu(x):
    return x * jax.nn.sigmoid(x)


def _dot(a, b):
    return jnp.dot(a, b, preferred_element_type=F32)


def _dot_nt(a, b):
    return lax.dot_general(a, b, (((1,), (1,)), ((), ())), preferred_element_type=F32)


def _split(x):
    hi = x.astype(BF16)
    return hi, (x - hi.astype(F32)).astype(BF16)


def _cumsum_dot_right(x, m):
    hi, lo = _split(x)
    return _dot(hi, m) + _dot(lo, m)


def _cumsum_dot_left(m, x):
    hi, lo = _split(x)
    return _dot(m, hi) + _dot(m, lo)


def _params(*sem):
    return pltpu.CompilerParams(dimension_semantics=sem, vmem_limit_bytes=VMEM_LIMIT)


def _layer_spec(a, l):
    return pl.BlockSpec((None,) + a.shape[1:], lambda *_: (l,) + (0,) * (a.ndim - 1))


def _in_proj_kernel(x_ref, g_ref, w_ref, wif_ref, *rest, kv_out):
    if kv_out:
        proj_ref, pif_ref, xn_ref, k_ref, v_ref = rest[-5:]
    else:
        proj_ref, pif_ref, xn_ref = rest
    j = pl.program_id(1)

    @pl.when(j == 0)
    def _():
        x = x_ref[...]
        ms = jnp.mean(x * x, axis=-1, keepdims=True)
        xn = (x * lax.rsqrt(ms + EPS) * g_ref[...]).astype(BF16)
        xn_ref[...] = xn
        pif_ref[...] = _dot(xn, wif_ref[...])

    proj = _dot(xn_ref[...], w_ref[...])
    proj_ref[...] = proj
    if kv_out:
        tm = proj.shape[0]
        for grp, ref in ((K_GROUP, k_ref), (V_GROUP, v_ref)):
            @pl.when(j == grp)
            def _(ref=ref):
                for h in range(HEADS):
                    ref[pl.ds(h, tm, stride=HEADS), :] = proj[:, h * LANE:(h + 1) * LANE]


def _in_proj(x, g, w_in, w_if, l, tm, kv_prev=None, kv_out=False):
    rows, d = x.shape
    depth = w_in.shape[0]
    tn = d // N_BRANCH
    n = 12 * tn
    n_i, n_j = rows // tm, n // tn
    in_specs = [pl.BlockSpec((tm, d), lambda i, j: (jnp.minimum(i + (j >= n_j // 2), n_i - 1), 0)),
                _layer_spec(g, l),
                pl.BlockSpec((None, d, tn), lambda i, j: (l, 0, j)),
                _layer_spec(w_if, l)]
    out_specs = [pl.BlockSpec((tm, tn), lambda i, j: (i, j)),
                 pl.BlockSpec((tm, LANE), lambda i, j: (i, 0)),
                 pl.BlockSpec((tm, d), lambda i, j: (i, 0))]
    out_shape = [SDS((rows, n), F32), SDS((rows, LANE), F32), SDS((rows, d), BF16)]
    args = [x, g, w_in, w_if]
    aliases = {}
    if kv_out:
        kv_spec = pl.BlockSpec((None, tm * HEADS, LANE), lambda i, j: (l, i, 0))
        out_specs += [kv_spec, kv_spec]
        out_shape += [SDS((depth, rows * HEADS, LANE), F32)] * 2
        if kv_prev is not None:
            in_specs += [pl.BlockSpec(memory_space=pl.ANY)] * 2
            args += list(kv_prev)
            aliases = {4: 3, 5: 4}
    return pl.pallas_call(
        functools.partial(_in_proj_kernel, kv_out=kv_out),
        grid=(rows // tm, n // tn),
        in_specs=in_specs,
        out_specs=out_specs,
        out_shape=out_shape,
        input_output_aliases=aliases,
        compiler_params=_params("parallel", "arbitrary"),
        name="in_proj",
    )(*args)


def _gmlp_kernel(u_ref, v_ref, z_ref, lg_ref, lb_ref, ws_ref, bst_ref, y_ref, *vn_out, n_chunks):
    v = v_ref[...]
    vc = v - jnp.mean(v, axis=-1, keepdims=True)
    var = jnp.mean(vc * vc, axis=-1, keepdims=True)
    vn = vc * lax.rsqrt(var + EPS) * lg_ref[...] + lb_ref[...]
    if vn_out:
        vn_out[0][...] = vn
    vnb = vn.astype(BF16)
    row = lax.broadcasted_iota(jnp.int32, (CHUNK, CHUNK), 0)
    col = lax.broadcasted_iota(jnp.int32, (CHUNK, CHUNK), 1)
    for g in range(HEADS):
        gs = slice(g * LANE, (g + 1) * LANE)
        wm = jnp.where(row >= col, ws_ref[g], 0.0).astype(BF16)
        bcol = bst_ref[:, g:g + 1]
        for c in range(n_chunks):
            ts = slice(c * CHUNK, (c + 1) * CHUNK)
            s = _dot(wm, vnb[ts, gs]) + bcol
            y_ref[ts, gs] = (u_ref[ts, gs] * s * _silu(z_ref[ts, gs])).astype(BF16)


def _gmlp(proj3, P, l, tt, emit_vn):
    b, t, _ = proj3.shape
    w = P['ln_g'].shape[-1]
    col = lambda c: pl.BlockSpec((None, tt, w), lambda i, j: (i, j, c))
    out_specs = [pl.BlockSpec((None, tt, w), lambda i, j: (i, j, 0))]
    out_shape = [SDS((b, t, w), BF16)]
    if emit_vn:
        out_specs.append(pl.BlockSpec((None, tt, w), lambda i, j: (i, j, 0)))
        out_shape.append(SDS((b, t, w), F32))
    params = [P[k] for k in ('ln_g', 'ln_b', 'ws', 'bs_t')]
    return pl.pallas_call(
        functools.partial(_gmlp_kernel, n_chunks=tt // CHUNK),
        grid=(b, t // tt),
        in_specs=[col(0), col(1), col(2)] + [_layer_spec(a, l) for a in params],
        out_specs=out_specs,
        out_shape=out_shape,
        compiler_params=_params("parallel", "parallel"),
        name="gmlp",
    )(proj3, proj3, proj3, *params)


def _causal_conv(buf_ref, x, cw_ref, cb_ref, tt):
    h0 = CONV_PAD - (CONV_W - 1)
    buf_ref[CONV_PAD:CONV_PAD + tt, :] = x
    y = cb_ref[...] + cw_ref[0:1, :] * buf_ref[h0:h0 + tt, :]
    for j in range(1, CONV_W):
        y = y + cw_ref[j:j + 1, :] * buf_ref[h0 + j:h0 + j + tt, :]
    buf_ref[h0:CONV_PAD, :] = x[tt - (CONV_W - 1):tt, :]
    return y


def _state_spec(a, sl):
    return pl.BlockSpec((None, None) + a.shape[2:], lambda i, j: (sl, i) + (0,) * (a.ndim - 2))


def _rglru_kernel(x_ref, z_ref, hist_ref, h0_ref, cw_ref, cb_ref, wa_ref, ba_ref, wx_ref, bx_ref, lam_ref,
                  y_ref, hist_out_ref, hlast_ref, buf_ref, hcar_ref, *, tt, t_valid, first_pos_zero):
    t = pl.program_id(1)
    h0 = CONV_PAD - (CONV_W - 1)

    @pl.when(t == 0)
    def _():
        buf_ref[h0:CONV_PAD, :] = hist_ref[...]
        hcar_ref[...] = h0_ref[...]

    x = x_ref[...]
    xc = _causal_conv(buf_ref, x, cw_ref, cb_ref, tt)
    xcb = xc.astype(BF16)
    ra, rx = [], []
    for blk in range(HEADS):
        bs = slice(blk * LANE, (blk + 1) * LANE)
        ra.append(_dot(xcb[:, bs], wa_ref[blk]))
        rx.append(_dot(xcb[:, bs], wx_ref[blk]))
    r = jax.nn.sigmoid(jnp.concatenate(ra, axis=-1) + ba_ref[...])
    i = jax.nn.sigmoid(jnp.concatenate(rx, axis=-1) + bx_ref[...])
    log_a = LRU_C * r * _log_sigmoid(lam_ref[...])
    a = jnp.exp(log_a)
    mult = jnp.sqrt(-jnp.tanh(log_a) * (a * a + 1.0))
    rowi = lax.broadcasted_iota(jnp.int32, (tt, 1), 0)
    if first_pos_zero:
        mult = jnp.where(rowi + t * tt == 0, 1.0, mult)
    u = mult * (i * xc)
    d = 1
    while d < tt:
        if d % SUBLANE == 0:
            u = jnp.concatenate([u[:d], a[d:] * u[:tt - d] + u[d:]], axis=0)
            a = jnp.concatenate([a[:d], a[d:] * a[:tt - d]], axis=0)
        else:
            keep = rowi >= d
            u = a * jnp.where(keep, pltpu.roll(u, d, axis=0), 0.0) + u
            a = a * jnp.where(keep, pltpu.roll(a, d, axis=0), 1.0)
        d *= 2
    h = a * hcar_ref[...] + u
    hcar_ref[...] = h[tt - 1:tt, :]
    y_ref[...] = (h * _silu(z_ref[...])).astype(BF16)

    @pl.when(t == pl.num_programs(1) - 1)
    def _():
        hist_out_ref[...] = x[t_valid - (CONV_W - 1):t_valid, :]
        hlast_ref[...] = h[t_valid - 1:t_valid, :]


def _rglru(proj3, st, sl, P, l, tt, t_valid, first_pos_zero):
    b, t, _ = proj3.shape
    w = P['lru_cb'].shape[-1]
    assert t_valid >= CONV_W - 1 and (t_valid == tt or t == tt)
    col = lambda c: pl.BlockSpec((None, tt, w), lambda i, j: (i, j, c))
    per_b = lambda r: pl.BlockSpec((None, r, w), lambda i, j: (i, 0, 0))
    params = [P[k] for k in ('lru_cw', 'lru_cb', 'lru_wa', 'lru_ba', 'lru_wx', 'lru_bx', 'lru_lam')]
    return pl.pallas_call(
        functools.partial(_rglru_kernel, tt=tt, t_valid=t_valid, first_pos_zero=first_pos_zero),
        grid=(b, t // tt),
        in_specs=[col(3), col(4), _state_spec(st['conv_b'], sl), _state_spec(st['h_b'], sl)]
                 + [_layer_spec(a, l) for a in params],
        out_specs=[pl.BlockSpec((None, tt, w), lambda i, j: (i, j, 0)), per_b(CONV_W - 1), per_b(1)],
        out_shape=[SDS((b, t, w), BF16), SDS((b, CONV_W - 1, w), F32), SDS((b, 1, w), F32)],
        scratch_shapes=[pltpu.VMEM((CONV_PAD + tt, w), F32), pltpu.VMEM((1, w), F32)],
        compiler_params=_params("parallel", "arbitrary"),
        name="rglru",
    )(proj3, proj3, st['conv_b'], st['h_b'], *params)


def _later_matrix(n):
    row = lax.broadcasted_iota(jnp.int32, (n, n), 0)
    col = lax.broadcasted_iota(jnp.int32, (n, n), 1)
    return (row > col).astype(BF16)


def _sb_logits(z):
    log_b = jnp.minimum(z, 0.0) - jnp.log(1.0 + jnp.exp(-jnp.abs(z)))
    return log_b, log_b - z


def _sb_span(qb, kspan, vspan, r, later, bias, scale, mask):
    cb = later.shape[0]
    log_b, log_1mb = _sb_logits(_dot_nt(qb, kspan) * scale + bias)
    if mask is not None:
        log_1mb = jnp.where(mask, log_1mb, 0.0)
    l1b = log_1mb.astype(BF16)
    blocks = [slice(j * cb, (j + 1) * cb) for j in range(kspan.shape[0] // cb)]
    suffix = [_dot(l1b[:, js], later) for js in blocks]
    atts = []
    for js, suf in zip(reversed(blocks), reversed(suffix)):
        att = jnp.exp(log_b[:, js] + suf + r)
        if mask is not None:
            att = jnp.where(mask[:, js], att, 0.0)
        atts.insert(0, att.astype(BF16))
        r = r + jnp.sum(log_1mb[:, js], axis=-1, keepdims=True)
    return _dot(jnp.concatenate(atts, axis=-1), vspan), r


def _sb_attn_kernel(bias_ref, q_ref, k_ref, v_ref, z_ref, y_ref, kb_ref, vb_ref, acc_ref, r_ref, *,
                    layer, tq, cb, scale):
    qi = pl.program_id(2)

    @pl.when(qi == 0)
    def _():
        kb_ref[...] = k_ref[...].astype(BF16)
        vb_ref[...] = v_ref[...].astype(BF16)

    bias = bias_ref[layer, pl.program_id(1)]
    later = _later_matrix(cb)
    qb = q_ref[...].astype(BF16)

    def span(k0, n, r, mask):
        return _sb_span(qb, kb_ref[pl.ds(k0, n), :], vb_ref[pl.ds(k0, n), :], r, later, bias, scale, mask)

    rr = lax.broadcasted_iota(jnp.int32, (tq, tq), 0)
    cc = lax.broadcasted_iota(jnp.int32, (tq, tq), 1)
    acc_ref[...], r_ref[...] = span(pl.multiple_of(qi * tq, tq), tq, jnp.zeros((tq, 1), F32), cc < rr)

    def body(i, carry):
        pv, r_ref[...] = span(pl.multiple_of((qi - 2 - 2 * i) * tq, tq), 2 * tq, r_ref[...], None)
        acc_ref[...] += pv
        return carry

    lax.fori_loop(0, qi // 2, body, 0)

    @pl.when(qi % 2 == 1)
    def _():
        pv, r_ref[...] = span(0, tq, r_ref[...], None)
        acc_ref[...] += pv

    y_ref[...] = (acc_ref[...] * _silu(z_ref[...])).astype(BF16)


def _sb_attn(proj3, bias, l, tq):
    b, t, _ = proj3.shape
    hd = LANE
    cb = min(tq, 2 * CHUNK)
    assert tq % cb == 0
    qcol, kcol, vcol, zcol = (5 * HEADS, K_GROUP * HEADS, V_GROUP * HEADS, 8 * HEADS)
    tile = lambda c0: pl.BlockSpec((None, tq, hd), lambda i, h, j: (i, j, c0 + h))
    whole = lambda c0: pl.BlockSpec((None, t, hd), lambda i, h, j: (i, 0, c0 + h))
    return pl.pallas_call(
        functools.partial(_sb_attn_kernel, layer=l, tq=tq, cb=cb, scale=hd ** -0.5),
        grid=(b, HEADS, t // tq),
        in_specs=[pl.BlockSpec(memory_space=pltpu.SMEM), tile(qcol), whole(kcol), whole(vcol), tile(zcol)],
        out_specs=pl.BlockSpec((None, tq, hd), lambda i, h, j: (i, j, h)),
        out_shape=SDS((b, t, HEADS * hd), BF16),
        scratch_shapes=[pltpu.VMEM((t, hd), BF16), pltpu.VMEM((t, hd), BF16),
                        pltpu.VMEM((tq, hd), F32), pltpu.VMEM((tq, 1), F32)],
        compiler_params=_params("parallel", "parallel", "arbitrary"),
        name="sb_attn",
    )(bias, proj3, proj3, proj3, proj3)


def _sb_decode_kernel(pt_ref, bias_ref, q_ref, kn_ref, vn_ref, z_ref, *rest, layer, n_pages, scale):
    k_pages, v_pages = rest[:n_pages], rest[n_pages:2 * n_pages]
    y_ref, acc_ref, r_ref = rest[2 * n_pages:]
    s = pl.program_id(1)
    later = _later_matrix(CHUNK)

    def update(keys, values, n_blk, masked):
        zs = []
        for h in range(HEADS):
            kh = jnp.concatenate([keys(h, g) for g in range(n_blk)], axis=0).astype(BF16)
            zh = _dot_nt(q_ref[:, h * LANE:(h + 1) * LANE].astype(BF16), kh) * scale + bias_ref[layer, h]
            zs += [zh[:, g * CHUNK:(g + 1) * CHUNK] for g in range(n_blk)]
        log_b, log_1mb = _sb_logits(jnp.concatenate(zs, axis=0))
        if masked:
            rr = lax.broadcasted_iota(jnp.int32, log_b.shape, 0) & (DEC_ROWS - 1)
            mask = lax.broadcasted_iota(jnp.int32, log_b.shape, 1) < rr
            log_1mb = jnp.where(mask, log_1mb, 0.0)
        suffix = _dot(log_1mb.astype(BF16), later)
        total = jnp.sum(log_1mb, axis=-1, keepdims=True)
        for h in range(HEADS):
            run = r_ref[h]
            atts = []
            for g in range(n_blk):
                rows = slice((h * n_blk + g) * DEC_ROWS, (h * n_blk + g + 1) * DEC_ROWS)
                att = jnp.exp(log_b[rows] + suffix[rows] + run)
                if masked:
                    att = jnp.where(mask[rows], att, 0.0)
                atts.append(att.astype(BF16))
                run = run + total[rows]
            vh = jnp.concatenate([values(h, g) for g in range(n_blk)], axis=0).astype(BF16)
            acc_ref[h] += _dot(jnp.concatenate(atts, axis=-1), vh)
            r_ref[h] = run

    @pl.when(s == 0)
    def _():
        acc_ref[...] = jnp.zeros_like(acc_ref)
        r_ref[...] = jnp.zeros_like(r_ref)
        update(lambda h, g: kn_ref[:, h * LANE:(h + 1) * LANE], lambda h, g: vn_ref[:, h * LANE:(h + 1) * LANE],
               1, True)

    update(lambda h, g: k_pages[g][pl.ds(h, CHUNK, stride=HEADS), :],
           lambda h, g: v_pages[g][pl.ds(h, CHUNK, stride=HEADS), :], n_pages, False)

    @pl.when(s == pl.num_programs(1) - 1)
    def _():
        acc = jnp.concatenate([acc_ref[h] for h in range(HEADS)], axis=-1)
        y_ref[...] = (acc * _silu(z_ref[...])).astype(BF16)


def _sb_decode(page_table, bias, q, k_new, v_new, z, cache_k, cache_v, layer, pages_per_step):
    b, n_tab = page_table.shape
    w = q.shape[-1]
    g = pages_per_step
    assert n_tab % g == 0
    page_rows, hd = cache_k.shape[2:]
    per_b = lambda r: pl.BlockSpec((None, r, w), lambda i, s, pt, bs: (i, 0, 0))

    def page_spec(j):
        return pl.BlockSpec((None, None, page_rows, hd),
                            lambda i, s, pt, bs: (layer, pt[i, n_tab - 1 - (s * g + j)], 0, 0))

    grid_spec = pltpu.PrefetchScalarGridSpec(
        num_scalar_prefetch=2,
        grid=(b, n_tab // g),
        in_specs=[per_b(DEC_ROWS), per_b(CHUNK), per_b(CHUNK), per_b(DEC_ROWS)]
                 + [page_spec(j) for j in range(g)] * 2,
        out_specs=per_b(DEC_ROWS),
        scratch_shapes=[pltpu.VMEM((HEADS, DEC_ROWS, hd), F32), pltpu.VMEM((HEADS, DEC_ROWS, hd), F32)],
    )
    return pl.pallas_call(
        functools.partial(_sb_decode_kernel, layer=layer, n_pages=g, scale=hd ** -0.5),
        grid_spec=grid_spec,
        out_shape=SDS((b, DEC_ROWS, w), BF16),
        compiler_params=_params("parallel", "arbitrary"),
        name="sb_decode",
    )(page_table, bias, q, k_new, v_new, z, *([cache_k] * g), *([cache_v] * g))


def _mlstm_kernel(x_ref, z_ref, o_ref, gc_ref, gr_ref, hist_ref, c0_ref, n0_ref, m0_ref,
                  cw_ref, cb_ref, wq_ref, wk_ref, wv_ref, brow_ref, bcol_ref, ng_ref, sk_ref,
                  y_ref, hist_out_ref, c_out_ref, n_out_ref, m_out_ref,
                  buf_ref, c_scr, n_scr, m_scr, *, t_valid, scale, n_chunks):
    step = pl.program_id(1)
    L = CHUNK
    rows = n_chunks * L
    h0 = CONV_PAD - (CONV_W - 1)

    @pl.when(step == 0)
    def _():
        buf_ref[h0:CONV_PAD, :] = hist_ref[...]
        c_scr[...] = c0_ref[...]
        n_scr[...] = n0_ref[...]
        m_scr[...] = m0_ref[...]

    x = x_ref[...]
    xc = _silu(_causal_conv(buf_ref, x, cw_ref, cb_ref, rows))
    xcb = xc.astype(BF16)
    xb = x.astype(BF16)
    gc = gc_ref[...] + brow_ref[...]
    gr = gr_ref[...] + bcol_ref[...]
    row = lax.broadcasted_iota(jnp.int32, (L, L), 0)
    col = lax.broadcasted_iota(jnp.int32, (L, L), 1)
    causal = row >= col
    tril, triu = causal.astype(BF16), (row <= col).astype(BF16)
    lsc, lsr = _log_sigmoid(gc), _log_sigmoid(gr)
    rowi = lax.broadcasted_iota(jnp.int32, (L, 1), 0)
    H = range(HEADS)
    CH = [(ci, h) for ci in range(n_chunks) for h in H]
    rs = [slice(ci * L, (ci + 1) * L) for ci in range(n_chunks)]
    hs = [slice(h * LANE, (h + 1) * LANE) for h in H]
    n_all, m_all = n_scr[...], m_scr[...]
    bc_all = [_cumsum_dot_left(tril, lsc[rs[ci], :]) for ci in range(n_chunks)]
    br_all = [_cumsum_dot_right(lsr[:, rs[ci]], triu) for ci in range(n_chunks)]
    q = {p: _dot(xcb[rs[p[0]], hs[p[1]]], wq_ref[p[1]]) for p in CH}
    k = {p: _dot(xcb[rs[p[0]], hs[p[1]]], wk_ref[p[1]]) * scale for p in CH}
    vb = {p: _dot(xb[rs[p[0]], hs[p[1]]], wv_ref[p[1]]).astype(BF16) for p in CH}
    qb = {p: q[p].astype(BF16) for p in CH}
    qk = {p: _dot_nt(qb[p], k[p].astype(BF16)) for p in CH}
    ig_c = {(ci, h): gc[rs[ci], h:h + 1] for ci, h in CH}
    b_c = {(ci, h): bc_all[ci][:, HEADS + h:HEADS + h + 1] for ci, h in CH}
    dmat = {(ci, h): jnp.where(causal, b_c[ci, h] - br_all[ci][HEADS + h:HEADS + h + 1, :] + gr[h:h + 1, rs[ci]],
                               -jnp.inf) for ci, h in CH}
    dmax = {p: jnp.max(dmat[p], axis=-1, keepdims=True) for p in CH}
    m_h, m_new, decay, kw = {}, {}, {}, {}
    for ci, h in CH:
        m_h[ci, h] = m_all[h:h + 1, 0:1] if ci == 0 else m_new[ci - 1, h]
        b_last = b_c[ci, h][t_valid - 1:t_valid, :]
        g = b_last - b_c[ci, h] + ig_c[ci, h]
        if t_valid < L:
            g = jnp.where(rowi < t_valid, g, -jnp.inf)
        m_new[ci, h] = jnp.maximum(b_last + m_h[ci, h], jnp.max(g, axis=0, keepdims=True))
        kw[ci, h] = k[ci, h] * jnp.exp(g - m_new[ci, h])
        decay[ci, h] = jnp.exp(b_last + m_h[ci, h] - m_new[ci, h])
    inter = {p: b_c[p] + m_h[p] for p in CH}
    m_t = {p: jnp.maximum(dmax[p], inter[p]) for p in CH}
    s = {p: qk[p] * jnp.exp(dmat[p] - m_t[p]) for p in CH}
    w_inter = {p: jnp.exp(inter[p] - m_t[p]) for p in CH}
    sv = {p: _dot(s[p].astype(BF16), vb[p]) for p in CH}
    kv = {p: _dot(kw[p].T.astype(BF16), vb[p]) for p in CH}
    qc, qn = {}, {}
    c_cur = [c_scr[h] for h in H]
    n_cur = [n_all[h:h + 1, :] for h in H]
    for ci, h in CH:
        qc[ci, h] = _dot(qb[ci, h], c_cur[h].astype(BF16))
        qn[ci, h] = jnp.sum(q[ci, h] * n_cur[h], axis=-1, keepdims=True)
        c_cur[h] = decay[ci, h] * c_cur[h] + kv[ci, h]
        n_cur[h] = decay[ci, h] * n_cur[h] + jnp.sum(kw[ci, h], axis=0, keepdims=True)
    for h in H:
        c_scr[h] = c_cur[h]
    n_scr[...] = jnp.concatenate(n_cur, axis=0)
    m_scr[...] = jnp.concatenate([jnp.broadcast_to(m_new[n_chunks - 1, h], (1, LANE)) for h in H], axis=0)
    num = {p: sv[p] + w_inter[p] * qc[p] for p in CH}
    den = {p: jnp.sum(s[p], axis=-1, keepdims=True) + w_inter[p] * qn[p] for p in CH}
    hh = {p: num[p] / jnp.maximum(jnp.abs(den[p]), jnp.exp(-m_t[p])) for p in CH}
    hh = {(ci, h): jax.nn.sigmoid(o_ref[rs[ci], hs[h]]) * hh[ci, h] for ci, h in CH}
    hc = {p: hh[p] - jnp.mean(hh[p], axis=-1, keepdims=True) for p in CH}
    outs = {p: hc[p] * lax.rsqrt(jnp.mean(hc[p] * hc[p], axis=-1, keepdims=True) + EPS) for p in CH}
    hn = jnp.concatenate([jnp.concatenate([outs[ci, h] for h in H], axis=-1) for ci in range(n_chunks)], axis=0)
    hn = hn * ng_ref[...] + sk_ref[...] * xc
    y_ref[...] = (hn * _silu(z_ref[...])).astype(BF16)

    @pl.when(step == pl.num_programs(1) - 1)
    def _():
        last = (n_chunks - 1) * L + t_valid
        hist_out_ref[...] = x[last - (CONV_W - 1):last, :]
        c_out_ref[...] = c_scr[...]
        n_out_ref[...] = n_scr[...]
        m_out_ref[...] = m_scr[...]


def _mlstm(proj3, gates_col, gates_row, st, sl, P, l, t_valid):
    b, t, _ = proj3.shape
    w = P['ml_cb'].shape[-1]
    hd = w // HEADS
    L = CHUNK
    assert t_valid >= CONV_W - 1 and (t_valid == L or t == L)
    n_chunks = 2 if t % (2 * L) == 0 else 1
    rows = n_chunks * L
    col = lambda c: pl.BlockSpec((None, rows, w), lambda i, j: (i, j, c))
    per_b = lambda *s: pl.BlockSpec((None,) + s, lambda i, j: (i,) + (0,) * len(s))
    states = [st[k] for k in ('conv_d', 'C', 'n', 'm')]
    params = [P[k] for k in ('ml_cw', 'ml_cb', 'ml_wq', 'ml_wk', 'ml_wv', 'ml_brow', 'ml_bcol', 'ml_ng', 'ml_sk')]
    return pl.pallas_call(
        functools.partial(_mlstm_kernel, t_valid=t_valid, scale=hd ** -0.5, n_chunks=n_chunks),
        grid=(b, t // rows),
        in_specs=[col(9), col(10), col(11),
                  pl.BlockSpec((None, rows, LANE), lambda i, j: (i, j, 0)),
                  pl.BlockSpec((None, 2 * HEADS, rows), lambda i, j: (i, 0, j))]
                 + [_state_spec(a, sl) for a in states] + [_layer_spec(a, l) for a in params],
        out_specs=[pl.BlockSpec((None, rows, w), lambda i, j: (i, j, 0)),
                   per_b(CONV_W - 1, w), per_b(HEADS, hd, hd), per_b(HEADS, hd), per_b(HEADS, LANE)],
        out_shape=[SDS((b, t, w), BF16), SDS((b, CONV_W - 1, w), F32), SDS((b, HEADS, hd, hd), F32),
                   SDS((b, HEADS, hd), F32), SDS((b, HEADS, LANE), F32)],
        scratch_shapes=[pltpu.VMEM((CONV_PAD + rows, w), F32), pltpu.VMEM((HEADS, hd, hd), F32),
                        pltpu.VMEM((HEADS, hd), F32), pltpu.VMEM((HEADS, LANE), F32)],
        compiler_params=_params("parallel", "arbitrary"),
        name="mlstm",
    )(proj3, proj3, proj3, gates_col, gates_row, *states, *params)


def _merge_kernel(xn_ref, a_ref, b_ref, c_ref, d_ref, wg0_ref, wg1_ref, wg2_ref, wg3_ref, bg_ref, wb_ref, o_ref):
    xn = xn_ref[...]
    acc = None
    branches = (a_ref, b_ref, c_ref, d_ref)
    gates = (wg0_ref, wg1_ref, wg2_ref, wg3_ref)
    for m in range(N_BRANCH):
        gate = jax.nn.sigmoid(_dot(xn, gates[m][...]) + bg_ref[m:m + 1, :])
        term = gate * _dot(branches[m][...], wb_ref[m])
        acc = term if acc is None else acc + term
    o_ref[...] = acc.astype(BF16)


def _merge(xn, branches, P, l, tm, tn):
    rows, d = xn.shape
    w = branches[0].shape[-1]
    nj = d // tn
    gate_spec = lambda m: pl.BlockSpec((None, d, tn), lambda i, j: (l, 0, m * nj + j))
    return pl.pallas_call(
        _merge_kernel,
        grid=(rows // tm, nj),
        in_specs=[pl.BlockSpec((tm, d), lambda i, j: (i, 0))]
                 + [pl.BlockSpec((tm, w), lambda i, j: (i, 0))] * N_BRANCH
                 + [gate_spec(m) for m in range(N_BRANCH)]
                 + [pl.BlockSpec((None, N_BRANCH, tn), lambda i, j: (l, 0, j)),
                    pl.BlockSpec((None, N_BRANCH, w, tn), lambda i, j: (l, 0, 0, j))],
        out_specs=pl.BlockSpec((tm, tn), lambda i, j: (i, j)),
        out_shape=SDS((rows, d), BF16),
        compiler_params=_params("parallel", "arbitrary"),
        name="merge",
    )(xn, *branches, *([P['w_gate']] * N_BRANCH), P['b_gate'], P['w_branch'])


def _out_proj_kernel(m_ref, x_ref, w_ref, g_ref, o_ref):
    out = _dot(m_ref[...], w_ref[...])
    ms = jnp.mean(out * out, axis=-1, keepdims=True)
    o_ref[...] = x_ref[...] + out * lax.rsqrt(ms + EPS) * g_ref[...]


def _out_proj(merged, x, P, l, tm):
    rows, d = x.shape
    return pl.pallas_call(
        _out_proj_kernel,
        grid=(rows // tm,),
        in_specs=[pl.BlockSpec((tm, d), lambda i: (i, 0)),
                  pl.BlockSpec((tm, d), lambda i: (i, 0)),
                  _layer_spec(P['w_out'], l),
                  _layer_spec(P['norm_post'], l)],
        out_specs=pl.BlockSpec((tm, d), lambda i: (i, 0)),
        out_shape=SDS((rows, d), F32),
        compiler_params=_params("parallel"),
        name="out_proj",
    )(merged, x, P['w_out'], P['norm_post'])


def _tile(n, pref):
    return pref if n % pref == 0 else n


def _branches(proj3, gates_col, P, l, st, sl, t_valid, first_pos_zero, emit_vn):
    b, t, _ = proj3.shape
    gm = _gmlp(proj3, P, l, _tile(t, 4 * CHUNK), emit_vn)
    tt = _tile(t, 4 * CHUNK) if t_valid == CHUNK else t
    y_b, hist_b, h_b = _rglru(proj3, st, sl, P, l, tt, tt if t_valid == CHUNK else t_valid, first_pos_zero)
    gates_row = jnp.swapaxes(gates_col[:, :, :2 * HEADS], 1, 2)
    y_d, hist_d, c, n, m = _mlstm(proj3, gates_col, gates_row, st, sl, P, l, t_valid)
    new_st = dict(conv_b=hist_b, h_b=h_b[:, 0], conv_d=hist_d, C=c, n=n, m=m[:, :, 0])
    return gm, y_b, y_d, new_st


def _dense_tail(x, xn, branches, P, l, tm):
    merged = _merge(xn, branches, P, l, tm, _tile(x.shape[1], 512))
    return _out_proj(merged, x, P, l, _tile(x.shape[0], 256))


def _state_in(conv_b, h_b, conv_d, c, n, m):
    return dict(conv_b=conv_b, h_b=h_b[:, :, None, :], conv_d=conv_d, C=c, n=n,
                m=jnp.broadcast_to(m[..., None], m.shape + (LANE,)))


def kernel(x_prompt, x_sample, cache_k, cache_v, page_table, state_rglru_conv, state_rglru_h, state_mlstm_conv, state_mlstm_c, state_mlstm_n, state_mlstm_m, norm_pre, norm_post, w_in, gmlp_ln_g, gmlp_ln_b, gmlp_ws, gmlp_bs, lru_conv_w, lru_conv_b, lru_wa, lru_ba, lru_wx, lru_bx, lru_lambda, ml_conv_w, ml_conv_b, ml_wq, ml_wk, ml_wv, ml_bi, ml_bf, ml_norm_g, ml_skip, sb_bias, w_branch, w_gate, b_gate, w_out):
    bp, tp, d = x_prompt.shape
    bs, ts, _ = x_sample.shape
    depth = w_in.shape[0]
    w = d // N_BRANCH
    hd = w // HEADS
    n_main = 12 * w
    page = cache_k.shape[2]
    past_len = page_table.shape[1] * page
    assert tp % CHUNK == 0 and page == CHUNK and ts <= DEC_ROWS and cache_k.shape[3:] == (HEADS, hd)
    ck = cache_k.reshape(depth, cache_k.shape[1], page * HEADS, hd)
    cv = cache_v.reshape(depth, cache_v.shape[1], page * HEADS, hd)
    n_tab = page_table.shape[1]
    pages_per_step = next(g for g in (16, 8, 4, 2, 1) if n_tab % g == 0)

    row3 = lambda a: a[:, None, :]
    gate_bias = jnp.concatenate([ml_bi, ml_bf], axis=-1)
    P = dict(
        norm_pre=row3(norm_pre), norm_post=row3(norm_post),
        w_in=w_in.astype(BF16),
        w_if=jnp.pad(w_in[:, :, n_main:], ((0, 0), (0, 0), (0, LANE - 2 * HEADS))).astype(BF16),
        ln_g=row3(gmlp_ln_g), ln_b=row3(gmlp_ln_b), ws=gmlp_ws, bs_t=jnp.swapaxes(gmlp_bs, 1, 2),
        lru_cw=lru_conv_w, lru_cb=row3(lru_conv_b), lru_wa=lru_wa.astype(BF16), lru_ba=row3(lru_ba),
        lru_wx=lru_wx.astype(BF16), lru_bx=row3(lru_bx), lru_lam=row3(lru_lambda),
        ml_cw=ml_conv_w, ml_cb=row3(ml_conv_b), ml_wq=ml_wq.astype(BF16), ml_wk=ml_wk.astype(BF16),
        ml_wv=ml_wv.astype(BF16), ml_brow=row3(jnp.pad(gate_bias, ((0, 0), (0, LANE - 2 * HEADS)))),
        ml_bcol=gate_bias[:, :, None], ml_ng=row3(ml_norm_g), ml_sk=row3(ml_skip),
        w_gate=w_gate.astype(BF16), b_gate=b_gate.reshape(depth, N_BRANCH, d),
        w_branch=w_branch.astype(BF16), w_out=w_out.astype(BF16))

    xp = x_prompt.reshape(bp * tp, d)
    xs = x_sample.reshape(bs * ts, d)
    st_p = _state_in(jnp.zeros((1, bp, CONV_W - 1, w), F32), jnp.zeros((1, bp, w), F32),
                     jnp.zeros((1, bp, CONV_W - 1, w), F32), jnp.zeros((1, bp, HEADS, hd, hd), F32),
                     jnp.zeros((1, bp, HEADS, hd), F32), jnp.zeros((1, bp, HEADS), F32))
    st_s = _state_in(state_rglru_conv, state_rglru_h, state_mlstm_conv, state_mlstm_c, state_mlstm_n, state_mlstm_m)
    names = ('conv_b', 'h_b', 'conv_d', 'C', 'n', 'm')
    res_p = {k: [] for k in names}
    res_s = {k: [] for k in names}
    ks_l, vs_l, gv_l = [], [], []
    kv_p = [jnp.zeros((depth, bp * tp * HEADS, LANE), F32) for _ in range(2)]
    pad_t = lambda a, rows: jnp.pad(a, ((0, 0), (0, rows - ts), (0, 0)))

    for l in range(depth):
        proj, pif, xn, *kv_p = _in_proj(xp, P['norm_pre'], P['w_in'], P['w_if'], l, _tile(bp * tp, 1024),
                                        kv_prev=kv_p, kv_out=True)
        proj3 = proj.reshape(bp, tp, n_main)
        y_a, y_b, y_d, nst = _branches(proj3, pif.reshape(bp, tp, LANE), P, l, st_p, 0, CHUNK, True, False)
        y_c = _sb_attn(proj3, sb_bias, l, _tile(tp, 4 * CHUNK))
        xp = _dense_tail(xp, xn, [y.reshape(bp * tp, w) for y in (y_a[0], y_b, y_c, y_d)], P, l,
                         _tile(bp * tp, 1024))
        for k in names:
            res_p[k].append(nst[k])

        proj, pif, xn = _in_proj(xs, P['norm_pre'], P['w_in'], P['w_if'], l, bs * ts)
        proj3 = proj.reshape(bs, ts, n_main)
        projc = pad_t(proj3, CHUNK)
        (y_a, vn), y_b, y_d, nst = _branches(projc, pad_t(pif.reshape(bs, ts, LANE), CHUNK), P, l, st_s, l, ts,
                                             past_len == 0, True)
        y_c = _sb_decode(page_table, sb_bias, pad_t(proj3[:, :, 5 * w:6 * w], DEC_ROWS),
                         projc[:, :, K_GROUP * w:(K_GROUP + 1) * w], projc[:, :, V_GROUP * w:(V_GROUP + 1) * w],
                         pad_t(proj3[:, :, 8 * w:9 * w], DEC_ROWS), ck, cv, l, pages_per_step)
        xs = _dense_tail(xs, xn, [y[:, :ts].reshape(bs * ts, w) for y in (y_a, y_b, y_c, y_d)], P, l, bs * ts)
        for k in names:
            res_s[k].append(nst[k])
        ks_l.append(proj3[:, :, K_GROUP * w:(K_GROUP + 1) * w].reshape(bs, ts, HEADS, hd))
        vs_l.append(proj3[:, :, V_GROUP * w:(V_GROUP + 1) * w].reshape(bs, ts, HEADS, hd))
        gv_l.append(vn[:, :ts])

    st = lambda lst: jnp.stack(lst, axis=0)
    k_p, v_p = (a.reshape(depth, bp, tp, HEADS, hd) for a in kv_p)
    return (xp.reshape(bp, tp, d), xs.reshape(bs, ts, d), k_p, v_p, st(ks_l), st(vs_l),
            st(res_p['conv_b']), st(res_p['h_b']), st(res_s['conv_b']), st(res_s['h_b']),
            st(res_p['conv_d']), st(res_p['C']), st(res_p['n']), st(res_p['m']),
            st(res_s['conv_d']), st(res_s['C']), st(res_s['n']), st(res_s['m']),
            st(gv_l))
```

```python
import functools

import jax
import jax.numpy as jnp
from jax import lax
from jax.experimental import pallas as pl
from jax.experimental.pallas import tpu as pltpu

F32 = jnp.float32
BF16 = jnp.bfloat16
SDS = jax.ShapeDtypeStruct

EPS = 1e-6
N_BRANCH = 4
HEADS = 4
CONV_W = 4
CHUNK = 128
LRU_C = 8.0
LANE = 128
SUBLANE = 8
CONV_PAD = SUBLANE
DEC_ROWS = 16
K_GROUP, V_GROUP = 6, 7
VMEM_LIMIT = 56 * 1024 * 1024


def _log_sigmoid(z):
    return jnp.minimum(z, 0.0) - jnp.log1p(jnp.exp(-jnp.abs(z)))


def _silu(x):
    return x * jax.nn.sigmoid(x)


def _dot(a, b):
    return jnp.dot(a, b, preferred_element_type=F32)


def _dot_nt(a, b):
    return lax.dot_general(a, b, (((1,), (1,)), ((), ())), preferred_element_type=F32)


def _split(x):
    hi = x.astype(BF16)
    return hi, (x - hi.astype(F32)).astype(BF16)


def _cumsum_dot_right(x, m):
    hi, lo = _split(x)
    return _dot(hi, m) + _dot(lo, m)


def _cumsum_dot_left(m, x):
    hi, lo = _split(x)
    return _dot(m, hi) + _dot(m, lo)


def _params(*sem):
    return pltpu.CompilerParams(dimension_semantics=sem, vmem_limit_bytes=VMEM_LIMIT)


def _layer_spec(a, l):
    return pl.BlockSpec((None,) + a.shape[1:], lambda *_: (l,) + (0,) * (a.ndim - 1))


def _in_proj_kernel(x_ref, g_ref, w_ref, wif_ref, *rest, kv_out):
    if kv_out:
        proj_ref, pif_ref, xn_ref, k_ref, v_ref = rest[-5:]
    else:
        proj_ref, pif_ref, xn_ref = rest
    j = pl.program_id(1)

    @pl.when(j == 0)
    def _():
        x = x_ref[...]
        ms = jnp.mean(x * x, axis=-1, keepdims=True)
        xn = (x * lax.rsqrt(ms + EPS) * g_ref[...]).astype(BF16)
        xn_ref[...] = xn
        pif_ref[...] = _dot(xn, wif_ref[...])

    proj = _dot(xn_ref[...], w_ref[...])
    proj_ref[...] = proj
    if kv_out:
        tm = proj.shape[0]
        for grp, ref in ((K_GROUP, k_ref), (V_GROUP, v_ref)):
            @pl.when(j == grp)
            def _(ref=ref):
                for h in range(HEADS):
                    ref[pl.ds(h, tm, stride=HEADS), :] = proj[:, h * LANE:(h + 1) * LANE]


def _in_proj(x, g, w_in, w_if, l, tm, kv_prev=None, kv_out=False):
    rows, d = x.shape
    depth = w_in.shape[0]
    tn = d // N_BRANCH
    n = 12 * tn
    n_i, n_j = rows // tm, n // tn
    in_specs = [pl.BlockSpec((tm, d), lambda i, j: (jnp.minimum(i + (j >= n_j // 2), n_i - 1), 0)),
                _layer_spec(g, l),
                pl.BlockSpec((None, d, tn), lambda i, j: (l, 0, j)),
                _layer_spec(w_if, l)]
    out_specs = [pl.BlockSpec((tm, tn), lambda i, j: (i, j)),
                 pl.BlockSpec((tm, LANE), lambda i, j: (i, 0)),
                 pl.BlockSpec((tm, d), lambda i, j: (i, 0))]
    out_shape = [SDS((rows, n), F32), SDS((rows, LANE), F32), SDS((rows, d), BF16)]
    args = [x, g, w_in, w_if]
    aliases = {}
    if kv_out:
        kv_spec = pl.BlockSpec((None, tm * HEADS, LANE), lambda i, j: (l, i, 0))
        out_specs += [kv_spec, kv_spec]
        out_shape += [SDS((depth, rows * HEADS, LANE), F32)] * 2
        if kv_prev is not None:
            in_specs += [pl.BlockSpec(memory_space=pl.ANY)] * 2
            args += list(kv_prev)
            aliases = {4: 3, 5: 4}
    return pl.pallas_call(
        functools.partial(_in_proj_kernel, kv_out=kv_out),
        grid=(rows // tm, n // tn),
        in_specs=in_specs,
        out_specs=out_specs,
        out_shape=out_shape,
        input_output_aliases=aliases,
        compiler_params=_params("parallel", "arbitrary"),
        name="in_proj",
    )(*args)


def _gmlp_kernel(u_ref, v_ref, z_ref, lg_ref, lb_ref, ws_ref, bst_ref, y_ref, *vn_out, n_chunks):
    v = v_ref[...]
    vc = v - jnp.mean(v, axis=-1, keepdims=True)
    var = jnp.mean(vc * vc, axis=-1, keepdims=True)
    vn = vc * lax.rsqrt(var + EPS) * lg_ref[...] + lb_ref[...]
    if vn_out:
        vn_out[0][...] = vn
    vnb = vn.astype(BF16)
    row = lax.broadcasted_iota(jnp.int32, (CHUNK, CHUNK), 0)
    col = lax.broadcasted_iota(jnp.int32, (CHUNK, CHUNK), 1)
    for g in range(HEADS):
        gs = slice(g * LANE, (g + 1) * LANE)
        wm = jnp.where(row >= col, ws_ref[g], 0.0).astype(BF16)
        bcol = bst_ref[:, g:g + 1]
        for c in range(n_chunks):
            ts = slice(c * CHUNK, (c + 1) * CHUNK)
            s = _dot(wm, vnb[ts, gs]) + bcol
            y_ref[ts, gs] = (u_ref[ts, gs] * s * _silu(z_ref[ts, gs])).astype(BF16)


def _gmlp(proj3, P, l, tt, emit_vn):
    b, t, _ = proj3.shape
    w = P['ln_g'].shape[-1]
    col = lambda c: pl.BlockSpec((None, tt, w), lambda i, j: (i, j, c))
    out_specs = [pl.BlockSpec((None, tt, w), lambda i, j: (i, j, 0))]
    out_shape = [SDS((b, t, w), BF16)]
    if emit_vn:
        out_specs.append(pl.BlockSpec((None, tt, w), lambda i, j: (i, j, 0)))
        out_shape.append(SDS((b, t, w), F32))
    params = [P[k] for k in ('ln_g', 'ln_b', 'ws', 'bs_t')]
    return pl.pallas_call(
        functools.partial(_gmlp_kernel, n_chunks=tt // CHUNK),
        grid=(b, t // tt),
        in_specs=[col(0), col(1), col(2)] + [_layer_spec(a, l) for a in params],
        out_specs=out_specs,
        out_shape=out_shape,
        compiler_params=_params("parallel", "parallel"),
        name="gmlp",
    )(proj3, proj3, proj3, *params)


def _causal_conv(buf_ref, x, cw_ref, cb_ref, tt):
    h0 = CONV_PAD - (CONV_W - 1)
    buf_ref[CONV_PAD:CONV_PAD + tt, :] = x
    y = cb_ref[...] + cw_ref[0:1, :] * buf_ref[h0:h0 + tt, :]
    for j in range(1, CONV_W):
        y = y + cw_ref[j:j + 1, :] * buf_ref[h0 + j:h0 + j + tt, :]
    buf_ref[h0:CONV_PAD, :] = x[tt - (CONV_W - 1):tt, :]
    return y


def _state_spec(a, sl):
    return pl.BlockSpec((None, None) + a.shape[2:], lambda i, j: (sl, i) + (0,) * (a.ndim - 2))


def _rglru_kernel(x_ref, z_ref, hist_ref, h0_ref, cw_ref, cb_ref, wa_ref, ba_ref, wx_ref, bx_ref, lam_ref,
                  y_ref, hist_out_ref, hlast_ref, buf_ref, hcar_ref, *, tt, t_valid, first_pos_zero):
    t = pl.program_id(1)
    h0 = CONV_PAD - (CONV_W - 1)

    @pl.when(t == 0)
    def _():
        buf_ref[h0:CONV_PAD, :] = hist_ref[...]
        hcar_ref[...] = h0_ref[...]

    x = x_ref[...]
    xc = _causal_conv(buf_ref, x, cw_ref, cb_ref, tt)
    xcb = xc.astype(BF16)
    ra, rx = [], []
    for blk in range(HEADS):
        bs = slice(blk * LANE, (blk + 1) * LANE)
        ra.append(_dot(xcb[:, bs], wa_ref[blk]))
        rx.append(_dot(xcb[:, bs], wx_ref[blk]))
    r = jax.nn.sigmoid(jnp.concatenate(ra, axis=-1) + ba_ref[...])
    i = jax.nn.sigmoid(jnp.concatenate(rx, axis=-1) + bx_ref[...])
    log_a = LRU_C * r * _log_sigmoid(lam_ref[...])
    a = jnp.exp(log_a)
    mult = jnp.sqrt(-jnp.tanh(log_a) * (a * a + 1.0))
    rowi = lax.broadcasted_iota(jnp.int32, (tt, 1), 0)
    if first_pos_zero:
        mult = jnp.where(rowi + t * tt == 0, 1.0, mult)
    u = mult * (i * xc)
    d = 1
    while d < tt:
        if d % SUBLANE == 0:
            u = jnp.concatenate([u[:d], a[d:] * u[:tt - d] + u[d:]], axis=0)
            a = jnp.concatenate([a[:d], a[d:] * a[:tt - d]], axis=0)
        else:
            keep = rowi >= d
            u = a * jnp.where(keep, pltpu.roll(u, d, axis=0), 0.0) + u
            a = a * jnp.where(keep, pltpu.roll(a, d, axis=0), 1.0)
        d *= 2
    h = a * hcar_ref[...] + u
    hcar_ref[...] = h[tt - 1:tt, :]
    y_ref[...] = (h * _silu(z_ref[...])).astype(BF16)

    @pl.when(t == pl.num_programs(1) - 1)
    def _():
        hist_out_ref[...] = x[t_valid - (CONV_W - 1):t_valid, :]
        hlast_ref[...] = h[t_valid - 1:t_valid, :]


def _rglru(proj3, st, sl, P, l, tt, t_valid, first_pos_zero):
    b, t, _ = proj3.shape
    w = P['lru_cb'].shape[-1]
    assert t_valid >= CONV_W - 1 and (t_valid == tt or t == tt)
    col = lambda c: pl.BlockSpec((None, tt, w), lambda i, j: (i, j, c))
    per_b = lambda r: pl.BlockSpec((None, r, w), lambda i, j: (i, 0, 0))
    params = [P[k] for k in ('lru_cw', 'lru_cb', 'lru_wa', 'lru_ba', 'lru_wx', 'lru_bx', 'lru_lam')]
    return pl.pallas_call(
        functools.partial(_rglru_kernel, tt=tt, t_valid=t_valid, first_pos_zero=first_pos_zero),
        grid=(b, t // tt),
        in_specs=[col(3), col(4), _state_spec(st['conv_b'], sl), _state_spec(st['h_b'], sl)]
                 + [_layer_spec(a, l) for a in params],
        out_specs=[pl.BlockSpec((None, tt, w), lambda i, j: (i, j, 0)), per_b(CONV_W - 1), per_b(1)],
        out_shape=[SDS((b, t, w), BF16), SDS((b, CONV_W - 1, w), F32), SDS((b, 1, w), F32)],
        scratch_shapes=[pltpu.VMEM((CONV_PAD + tt, w), F32), pltpu.VMEM((1, w), F32)],
        compiler_params=_params("parallel", "arbitrary"),
        name="rglru",
    )(proj3, proj3, st['conv_b'], st['h_b'], *params)


def _later_matrix(n):
    row = lax.broadcasted_iota(jnp.int32, (n, n), 0)
    col = lax.broadcasted_iota(jnp.int32, (n, n), 1)
    return (row > col).astype(BF16)


def _sb_logits(z):
    log_b = jnp.minimum(z, 0.0) - jnp.log(1.0 + jnp.exp(-jnp.abs(z)))
    return log_b, log_b - z


def _sb_span(qb, kspan, vspan, r, later, bias, scale, mask):
    cb = later.shape[0]
    log_b, log_1mb = _sb_logits(_dot_nt(qb, kspan) * scale + bias)
    if mask is not None:
        log_1mb = jnp.where(mask, log_1mb, 0.0)
    l1b = log_1mb.astype(BF16)
    blocks = [slice(j * cb, (j + 1) * cb) for j in range(kspan.shape[0] // cb)]
    suffix = [_dot(l1b[:, js], later) for js in blocks]
    atts = []
    for js, suf in zip(reversed(blocks), reversed(suffix)):
        att = jnp.exp(log_b[:, js] + suf + r)
        if mask is not None:
            att = jnp.where(mask[:, js], att, 0.0)
        atts.insert(0, att.astype(BF16))
        r = r + jnp.sum(log_1mb[:, js], axis=-1, keepdims=True)
    return _dot(jnp.concatenate(atts, axis=-1), vspan), r


def _sb_attn_kernel(bias_ref, q_ref, k_ref, v_ref, z_ref, y_ref, kb_ref, vb_ref, acc_ref, r_ref, *,
                    layer, tq, cb, scale):
    qi = pl.program_id(2)

    @pl.when(qi == 0)
    def _():
        kb_ref[...] = k_ref[...].astype(BF16)
        vb_ref[...] = v_ref[...].astype(BF16)

    bias = bias_ref[layer, pl.program_id(1)]
    later = _later_matrix(cb)
    qb = q_ref[...].astype(BF16)

    def span(k0, n, r, mask):
        return _sb_span(qb, kb_ref[pl.ds(k0, n), :], vb_ref[pl.ds(k0, n), :], r, later, bias, scale, mask)

    rr = lax.broadcasted_iota(jnp.int32, (tq, tq), 0)
    cc = lax.broadcasted_iota(jnp.int32, (tq, tq), 1)
    acc_ref[...], r_ref[...] = span(pl.multiple_of(qi * tq, tq), tq, jnp.zeros((tq, 1), F32), cc < rr)

    def body(i, carry):
        pv, r_ref[...] = span(pl.multiple_of((qi - 2 - 2 * i) * tq, tq), 2 * tq, r_ref[...], None)
        acc_ref[...] += pv
        return carry

    lax.fori_loop(0, qi // 2, body, 0)

    @pl.when(qi % 2 == 1)
    def _():
        pv, r_ref[...] = span(0, tq, r_ref[...], None)
        acc_ref[...] += pv

    y_ref[...] = (acc_ref[...] * _silu(z_ref[...])).astype(BF16)


def _sb_attn(proj3, bias, l, tq):
    b, t, _ = proj3.shape
    hd = LANE
    cb = min(tq, 2 * CHUNK)
    assert tq % cb == 0
    qcol, kcol, vcol, zcol = (5 * HEADS, K_GROUP * HEADS, V_GROUP * HEADS, 8 * HEADS)
    tile = lambda c0: pl.BlockSpec((None, tq, hd), lambda i, h, j: (i, j, c0 + h))
    whole = lambda c0: pl.BlockSpec((None, t, hd), lambda i, h, j: (i, 0, c0 + h))
    return pl.pallas_call(
        functools.partial(_sb_attn_kernel, layer=l, tq=tq, cb=cb, scale=hd ** -0.5),
        grid=(b, HEADS, t // tq),
        in_specs=[pl.BlockSpec(memory_space=pltpu.SMEM), tile(qcol), whole(kcol), whole(vcol), tile(zcol)],
        out_specs=pl.BlockSpec((None, tq, hd), lambda i, h, j: (i, j, h)),
        out_shape=SDS((b, t, HEADS * hd), BF16),
        scratch_shapes=[pltpu.VMEM((t, hd), BF16), pltpu.VMEM((t, hd), BF16),
                        pltpu.VMEM((tq, hd), F32), pltpu.VMEM((tq, 1), F32)],
        compiler_params=_params("parallel", "parallel", "arbitrary"),
        name="sb_attn",
    )(bias, proj3, proj3, proj3, proj3)


def _sb_decode_kernel(pt_ref, bias_ref, q_ref, kn_ref, vn_ref, z_ref, *rest, layer, n_pages, scale):
    k_pages, v_pages = rest[:n_pages], rest[n_pages:2 * n_pages]
    y_ref, acc_ref, r_ref = rest[2 * n_pages:]
    s = pl.program_id(1)
    later = _later_matrix(CHUNK)

    def update(keys, values, n_blk, masked):
        zs = []
        for h in range(HEADS):
            kh = jnp.concatenate([keys(h, g) for g in range(n_blk)], axis=0).astype(BF16)
            zh = _dot_nt(q_ref[:, h * LANE:(h + 1) * LANE].astype(BF16), kh) * scale + bias_ref[layer, h]
            zs += [zh[:, g * CHUNK:(g + 1) * CHUNK] for g in range(n_blk)]
        log_b, log_1mb = _sb_logits(jnp.concatenate(zs, axis=0))
        if masked:
            rr = lax.broadcasted_iota(jnp.int32, log_b.shape, 0) & (DEC_ROWS - 1)
            mask = lax.broadcasted_iota(jnp.int32, log_b.shape, 1) < rr
            log_1mb = jnp.where(mask, log_1mb, 0.0)
        suffix = _dot(log_1mb.astype(BF16), later)
        total = jnp.sum(log_1mb, axis=-1, keepdims=True)
        for h in range(HEADS):
            run = r_ref[h]
            atts = []
            for g in range(n_blk):
                rows = slice((h * n_blk + g) * DEC_ROWS, (h * n_blk + g + 1) * DEC_ROWS)
                att = jnp.exp(log_b[rows] + suffix[rows] + run)
                if masked:
                    att = jnp.where(mask[rows], att, 0.0)
                atts.append(att.astype(BF16))
                run = run + total[rows]
            vh = jnp.concatenate([values(h, g) for g in range(n_blk)], axis=0).astype(BF16)
            acc_ref[h] += _dot(jnp.concatenate(atts, axis=-1), vh)
            r_ref[h] = run

    @pl.when(s == 0)
    def _():
        acc_ref[...] = jnp.zeros_like(acc_ref)
        r_ref[...] = jnp.zeros_like(r_ref)
        update(lambda h, g: kn_ref[:, h * LANE:(h + 1) * LANE], lambda h, g: vn_ref[:, h * LANE:(h + 1) * LANE],
               1, True)

    update(lambda h, g: k_pages[g][pl.ds(h, CHUNK, stride=HEADS), :],
           lambda h, g: v_pages[g][pl.ds(h, CHUNK, stride=HEADS), :], n_pages, False)

    @pl.when(s == pl.num_programs(1) - 1)
    def _():
        acc = jnp.concatenate([acc_ref[h] for h in range(HEADS)], axis=-1)
        y_ref[...] = (acc * _silu(z_ref[...])).astype(BF16)


def _sb_decode(page_table, bias, q, k_new, v_new, z, cache_k, cache_v, layer, pages_per_step):
    b, n_tab = page_table.shape
    w = q.shape[-1]
    g = pages_per_step
    assert n_tab % g == 0
    page_rows, hd = cache_k.shape[2:]
    per_b = lambda r: pl.BlockSpec((None, r, w), lambda i, s, pt, bs: (i, 0, 0))

    def page_spec(j):
        return pl.BlockSpec((None, None, page_rows, hd),
                            lambda i, s, pt, bs: (layer, pt[i, n_tab - 1 - (s * g + j)], 0, 0))

    grid_spec = pltpu.PrefetchScalarGridSpec(
        num_scalar_prefetch=2,
        grid=(b, n_tab // g),
        in_specs=[per_b(DEC_ROWS), per_b(CHUNK), per_b(CHUNK), per_b(DEC_ROWS)]
                 + [page_spec(j) for j in range(g)] * 2,
        out_specs=per_b(DEC_ROWS),
        scratch_shapes=[pltpu.VMEM((HEADS, DEC_ROWS, hd), F32), pltpu.VMEM((HEADS, DEC_ROWS, hd), F32)],
    )
    return pl.pallas_call(
        functools.partial(_sb_decode_kernel, layer=layer, n_pages=g, scale=hd ** -0.5),
        grid_spec=grid_spec,
        out_shape=SDS((b, DEC_ROWS, w), BF16),
        compiler_params=_params("parallel", "arbitrary"),
        name="sb_decode",
    )(page_table, bias, q, k_new, v_new, z, *([cache_k] * g), *([cache_v] * g))


def _mlstm_kernel(x_ref, z_ref, o_ref, gc_ref, gr_ref, hist_ref, c0_ref, n0_ref, m0_ref,
                  cw_ref, cb_ref, wq_ref, wk_ref, wv_ref, brow_ref, bcol_ref, ng_ref, sk_ref,
                  y_ref, hist_out_ref, c_out_ref, n_out_ref, m_out_ref,
                  buf_ref, c_scr, n_scr, m_scr, *, t_valid, scale, n_chunks):
    step = pl.program_id(1)
    L = CHUNK
    rows = n_chunks * L
    h0 = CONV_PAD - (CONV_W - 1)

    @pl.when(step == 0)
    def _():
        buf_ref[h0:CONV_PAD, :] = hist_ref[...]
        c_scr[...] = c0_ref[...]
        n_scr[...] = n0_ref[...]
        m_scr[...] = m0_ref[...]

    x = x_ref[...]
    xc = _silu(_causal_conv(buf_ref, x, cw_ref, cb_ref, rows))
    xcb = xc.astype(BF16)
    xb = x.astype(BF16)
    gc = gc_ref[...] + brow_ref[...]
    gr = gr_ref[...] + bcol_ref[...]
    row = lax.broadcasted_iota(jnp.int32, (L, L), 0)
    col = lax.broadcasted_iota(jnp.int32, (L, L), 1)
    causal = row >= col
    tril, triu = causal.astype(BF16), (row <= col).astype(BF16)
    lsc, lsr = _log_sigmoid(gc), _log_sigmoid(gr)
    rowi = lax.broadcasted_iota(jnp.int32, (L, 1), 0)
    H = range(HEADS)
    CH = [(ci, h) for ci in range(n_chunks) for h in H]
    rs = [slice(ci * L, (ci + 1) * L) for ci in range(n_chunks)]
    hs = [slice(h * LANE, (h + 1) * LANE) for h in H]
    n_all, m_all = n_scr[...], m_scr[...]
    bc_all = [_cumsum_dot_left(tril, lsc[rs[ci], :]) for ci in range(n_chunks)]
    br_all = [_cumsum_dot_right(lsr[:, rs[ci]], triu) for ci in range(n_chunks)]
    q = {p: _dot(xcb[rs[p[0]], hs[p[1]]], wq_ref[p[1]]) for p in CH}
    k = {p: _dot(xcb[rs[p[0]], hs[p[1]]], wk_ref[p[1]]) * scale for p in CH}
    vb = {p: _dot(xb[rs[p[0]], hs[p[1]]], wv_ref[p[1]]).astype(BF16) for p in CH}
    qb = {p: q[p].astype(BF16) for p in CH}
    qk = {p: _dot_nt(qb[p], k[p].astype(BF16)) for p in CH}
    ig_c = {(ci, h): gc[rs[ci], h:h + 1] for ci, h in CH}
    b_c = {(ci, h): bc_all[ci][:, HEADS + h:HEADS + h + 1] for ci, h in CH}
    dmat = {(ci, h): jnp.where(causal, b_c[ci, h] - br_all[ci][HEADS + h:HEADS + h + 1, :] + gr[h:h + 1, rs[ci]],
                               -jnp.inf) for ci, h in CH}
    dmax = {p: jnp.max(dmat[p], axis=-1, keepdims=True) for p in CH}
    m_h, m_new, decay, kw = {}, {}, {}, {}
    for ci, h in CH:
        m_h[ci, h] = m_all[h:h + 1, 0:1] if ci == 0 else m_new[ci - 1, h]
        b_last = b_c[ci, h][t_valid - 1:t_valid, :]
        g = b_last - b_c[ci, h] + ig_c[ci, h]
        if t_valid < L:
            g = jnp.where(rowi < t_valid, g, -jnp.inf)
        m_new[ci, h] = jnp.maximum(b_last + m_h[ci, h], jnp.max(g, axis=0, keepdims=True))
        kw[ci, h] = k[ci, h] * jnp.exp(g - m_new[ci, h])
        decay[ci, h] = jnp.exp(b_last + m_h[ci, h] - m_new[ci, h])
    inter = {p: b_c[p] + m_h[p] for p in CH}
    m_t = {p: jnp.maximum(dmax[p], inter[p]) for p in CH}
    s = {p: qk[p] * jnp.exp(dmat[p] - m_t[p]) for p in CH}
    w_inter = {p: jnp.exp(inter[p] - m_t[p]) for p in CH}
    sv = {p: _dot(s[p].astype(BF16), vb[p]) for p in CH}
    kv = {p: _dot(kw[p].T.astype(BF16), vb[p]) for p in CH}
    qc, qn = {}, {}
    c_cur = [c_scr[h] for h in H]
    n_cur = [n_all[h:h + 1, :] for h in H]
    for ci, h in CH:
        qc[ci, h] = _dot(qb[ci, h], c_cur[h].astype(BF16))
        qn[ci, h] = jnp.sum(q[ci, h] * n_cur[h], axis=-1, keepdims=True)
        c_cur[h] = decay[ci, h] * c_cur[h] + kv[ci, h]
        n_cur[h] = decay[ci, h] * n_cur[h] + jnp.sum(kw[ci, h], axis=0, keepdims=True)
    for h in H:
        c_scr[h] = c_cur[h]
    n_scr[...] = jnp.concatenate(n_cur, axis=0)
    m_scr[...] = jnp.concatenate([jnp.broadcast_to(m_new[n_chunks - 1, h], (1, LANE)) for h in H], axis=0)
    num = {p: sv[p] + w_inter[p] * qc[p] for p in CH}
    den = {p: jnp.sum(s[p], axis=-1, keepdims=True) + w_inter[p] * qn[p] for p in CH}
    hh = {p: num[p] / jnp.maximum(jnp.abs(den[p]), jnp.exp(-m_t[p])) for p in CH}
    hh = {(ci, h): jax.nn.sigmoid(o_ref[rs[ci], hs[h]]) * hh[ci, h] for ci, h in CH}
    hc = {p: hh[p] - jnp.mean(hh[p], axis=-1, keepdims=True) for p in CH}
    outs = {p: hc[p] * lax.rsqrt(jnp.mean(hc[p] * hc[p], axis=-1, keepdims=True) + EPS) for p in CH}
    hn = jnp.concatenate([jnp.concatenate([outs[ci, h] for h in H], axis=-1) for ci in range(n_chunks)], axis=0)
    hn = hn * ng_ref[...] + sk_ref[...] * xc
    y_ref[...] = (hn * _silu(z_ref[...])).astype(BF16)

    @pl.when(step == pl.num_programs(1) - 1)
    def _():
        last = (n_chunks - 1) * L + t_valid
        hist_out_ref[...] = x[last - (CONV_W - 1):last, :]
        c_out_ref[...] = c_scr[...]
        n_out_ref[...] = n_scr[...]
        m_out_ref[...] = m_scr[...]


def _mlstm(proj3, gates_col, gates_row, st, sl, P, l, t_valid):
    b, t, _ = proj3.shape
    w = P['ml_cb'].shape[-1]
    hd = w // HEADS
    L = CHUNK
    assert t_valid >= CONV_W - 1 and (t_valid == L or t == L)
    n_chunks = 4 if t % (4 * L) == 0 else 1
    rows = n_chunks * L
    col = lambda c: pl.BlockSpec((None, rows, w), lambda i, j: (i, j, c))
    per_b = lambda *s: pl.BlockSpec((None,) + s, lambda i, j: (i,) + (0,) * len(s))
    states = [st[k] for k in ('conv_d', 'C', 'n', 'm')]
    params = [P[k] for k in ('ml_cw', 'ml_cb', 'ml_wq', 'ml_wk', 'ml_wv', 'ml_brow', 'ml_bcol', 'ml_ng', 'ml_sk')]
    return pl.pallas_call(
        functools.partial(_mlstm_kernel, t_valid=t_valid, scale=hd ** -0.5, n_chunks=n_chunks),
        grid=(b, t // rows),
        in_specs=[col(9), col(10), col(11),
                  pl.BlockSpec((None, rows, LANE), lambda i, j: (i, j, 0)),
                  pl.BlockSpec((None, 2 * HEADS, rows), lambda i, j: (i, 0, j))]
                 + [_state_spec(a, sl) for a in states] + [_layer_spec(a, l) for a in params],
        out_specs=[pl.BlockSpec((None, rows, w), lambda i, j: (i, j, 0)),
                   per_b(CONV_W - 1, w), per_b(HEADS, hd, hd), per_b(HEADS, hd), per_b(HEADS, LANE)],
        out_shape=[SDS((b, t, w), BF16), SDS((b, CONV_W - 1, w), F32), SDS((b, HEADS, hd, hd), F32),
                   SDS((b, HEADS, hd), F32), SDS((b, HEADS, LANE), F32)],
        scratch_shapes=[pltpu.VMEM((CONV_PAD + rows, w), F32), pltpu.VMEM((HEADS, hd, hd), F32),
                        pltpu.VMEM((HEADS, hd), F32), pltpu.VMEM((HEADS, LANE), F32)],
        compiler_params=_params("parallel", "arbitrary"),
        name="mlstm",
    )(proj3, proj3, proj3, gates_col, gates_row, *states, *params)


def _merge_kernel(xn_ref, a_ref, b_ref, c_ref, d_ref, wg0_ref, wg1_ref, wg2_ref, wg3_ref, bg_ref, wb_ref, o_ref):
    xn = xn_ref[...]
    acc = None
    branches = (a_ref, b_ref, c_ref, d_ref)
    gates = (wg0_ref, wg1_ref, wg2_ref, wg3_ref)
    for m in range(N_BRANCH):
        gate = jax.nn.sigmoid(_dot(xn, gates[m][...]) + bg_ref[m:m + 1, :])
        term = gate * _dot(branches[m][...], wb_ref[m])
        acc = term if acc is None else acc + term
    o_ref[...] = acc.astype(BF16)


def _merge(xn, branches, P, l, tm, tn):
    rows, d = xn.shape
    w = branches[0].shape[-1]
    nj = d // tn
    gate_spec = lambda m: pl.BlockSpec((None, d, tn), lambda i, j: (l, 0, m * nj + j))
    return pl.pallas_call(
        _merge_kernel,
        grid=(rows // tm, nj),
        in_specs=[pl.BlockSpec((tm, d), lambda i, j: (i, 0))]
                 + [pl.BlockSpec((tm, w), lambda i, j: (i, 0))] * N_BRANCH
                 + [gate_spec(m) for m in range(N_BRANCH)]
                 + [pl.BlockSpec((None, N_BRANCH, tn), lambda i, j: (l, 0, j)),
                    pl.BlockSpec((None, N_BRANCH, w, tn), lambda i, j: (l, 0, 0, j))],
        out_specs=pl.BlockSpec((tm, tn), lambda i, j: (i, j)),
        out_shape=SDS((rows, d), BF16),
        compiler_params=_params("parallel", "arbitrary"),
        name="merge",
    )(xn, *branches, *([P['w_gate']] * N_BRANCH), P['b_gate'], P['w_branch'])


def _out_proj_kernel(m_ref, x_ref, w_ref, g_ref, o_ref):
    out = _dot(m_ref[...], w_ref[...])
    ms = jnp.mean(out * out, axis=-1, keepdims=True)
    o_ref[...] = x_ref[...] + out * lax.rsqrt(ms + EPS) * g_ref[...]


def _out_proj(merged, x, P, l, tm):
    rows, d = x.shape
    return pl.pallas_call(
        _out_proj_kernel,
        grid=(rows // tm,),
        in_specs=[pl.BlockSpec((tm, d), lambda i: (i, 0)),
                  pl.BlockSpec((tm, d), lambda i: (i, 0)),
                  _layer_spec(P['w_out'], l),
                  _layer_spec(P['norm_post'], l)],
        out_specs=pl.BlockSpec((tm, d), lambda i: (i, 0)),
        out_shape=SDS((rows, d), F32),
        compiler_params=_params("parallel"),
        name="out_proj",
    )(merged, x, P['w_out'], P['norm_post'])


def _tile(n, pref):
    return pref if n % pref == 0 else n


def _branches(proj3, gates_col, P, l, st, sl, t_valid, first_pos_zero, emit_vn):
    b, t, _ = proj3.shape
    gm = _gmlp(proj3, P, l, _tile(t, 4 * CHUNK), emit_vn)
    tt = _tile(t, 4 * CHUNK) if t_valid == CHUNK else t
    y_b, hist_b, h_b = _rglru(proj3, st, sl, P, l, tt, tt if t_valid == CHUNK else t_valid, first_pos_zero)
    gates_row = jnp.swapaxes(gates_col[:, :, :2 * HEADS], 1, 2)
    y_d, hist_d, c, n, m = _mlstm(proj3, gates_col, gates_row, st, sl, P, l, t_valid)
    new_st = dict(conv_b=hist_b, h_b=h_b[:, 0], conv_d=hist_d, C=c, n=n, m=m[:, :, 0])
    return gm, y_b, y_d, new_st


def _dense_tail(x, xn, branches, P, l, tm):
    merged = _merge(xn, branches, P, l, tm, _tile(x.shape[1], 512))
    return _out_proj(merged, x, P, l, _tile(x.shape[0], 512))


def _state_in(conv_b, h_b, conv_d, c, n, m):
    return dict(conv_b=conv_b, h_b=h_b[:, :, None, :], conv_d=conv_d, C=c, n=n,
                m=jnp.broadcast_to(m[..., None], m.shape + (LANE,)))


def kernel(x_prompt, x_sample, cache_k, cache_v, page_table, state_rglru_conv, state_rglru_h, state_mlstm_conv, state_mlstm_c, state_mlstm_n, state_mlstm_m, norm_pre, norm_post, w_in, gmlp_ln_g, gmlp_ln_b, gmlp_ws, gmlp_bs, lru_conv_w, lru_conv_b, lru_wa, lru_ba, lru_wx, lru_bx, lru_lambda, ml_conv_w, ml_conv_b, ml_wq, ml_wk, ml_wv, ml_bi, ml_bf, ml_norm_g, ml_skip, sb_bias, w_branch, w_gate, b_gate, w_out):
    bp, tp, d = x_prompt.shape
    bs, ts, _ = x_sample.shape
    depth = w_in.shape[0]
    w = d // N_BRANCH
    hd = w // HEADS
    n_main = 12 * w
    page = cache_k.shape[2]
    past_len = page_table.shape[1] * page
    assert tp % CHUNK == 0 and page == CHUNK and ts <= DEC_ROWS and cache_k.shape[3:] == (HEADS, hd)
    ck = cache_k.reshape(depth, cache_k.shape[1], page * HEADS, hd)
    cv = cache_v.reshape(depth, cache_v.shape[1], page * HEADS, hd)
    n_tab = page_table.shape[1]
    pages_per_step = next(g for g in (16, 8, 4, 2, 1) if n_tab % g == 0)

    row3 = lambda a: a[:, None, :]
    gate_bias = jnp.concatenate([ml_bi, ml_bf], axis=-1)
    P = dict(
        norm_pre=row3(norm_pre), norm_post=row3(norm_post),
        w_in=w_in.astype(BF16),
        w_if=jnp.pad(w_in[:, :, n_main:], ((0, 0), (0, 0), (0, LANE - 2 * HEADS))).astype(BF16),
        ln_g=row3(gmlp_ln_g), ln_b=row3(gmlp_ln_b), ws=gmlp_ws, bs_t=jnp.swapaxes(gmlp_bs, 1, 2),
        lru_cw=lru_conv_w, lru_cb=row3(lru_conv_b), lru_wa=lru_wa.astype(BF16), lru_ba=row3(lru_ba),
        lru_wx=lru_wx.astype(BF16), lru_bx=row3(lru_bx), lru_lam=row3(lru_lambda),
        ml_cw=ml_conv_w, ml_cb=row3(ml_conv_b), ml_wq=ml_wq.astype(BF16), ml_wk=ml_wk.astype(BF16),
        ml_wv=ml_wv.astype(BF16), ml_brow=row3(jnp.pad(gate_bias, ((0, 0), (0, LANE - 2 * HEADS)))),
        ml_bcol=gate_bias[:, :, None], ml_ng=row3(ml_norm_g), ml_sk=row3(ml_skip),
        w_gate=w_gate.astype(BF16), b_gate=b_gate.reshape(depth, N_BRANCH, d),
        w_branch=w_branch.astype(BF16), w_out=w_out.astype(BF16))

    xp = x_prompt.reshape(bp * tp, d)
    xs = x_sample.reshape(bs * ts, d)
    st_p = _state_in(jnp.zeros((1, bp, CONV_W - 1, w), F32), jnp.zeros((1, bp, w), F32),
                     jnp.zeros((1, bp, CONV_W - 1, w), F32), jnp.zeros((1, bp, HEADS, hd, hd), F32),
                     jnp.zeros((1, bp, HEADS, hd), F32), jnp.zeros((1, bp, HEADS), F32))
    st_s = _state_in(state_rglru_conv, state_rglru_h, state_mlstm_conv, state_mlstm_c, state_mlstm_n, state_mlstm_m)
    names = ('conv_b', 'h_b', 'conv_d', 'C', 'n', 'm')
    res_p = {k: [] for k in names}
    res_s = {k: [] for k in names}
    ks_l, vs_l, gv_l = [], [], []
    kv_p = [jnp.zeros((depth, bp * tp * HEADS, LANE), F32) for _ in range(2)]
    pad_t = lambda a, rows: jnp.pad(a, ((0, 0), (0, rows - ts), (0, 0)))

    for l in range(depth):
        proj, pif, xn, *kv_p = _in_proj(xp, P['norm_pre'], P['w_in'], P['w_if'], l, _tile(bp * tp, 1024),
                                        kv_prev=kv_p, kv_out=True)
        proj3 = proj.reshape(bp, tp, n_main)
        y_a, y_b, y_d, nst = _branches(proj3, pif.reshape(bp, tp, LANE), P, l, st_p, 0, CHUNK, True, False)
        y_c = _sb_attn(proj3, sb_bias, l, _tile(tp, 4 * CHUNK))
        xp = _dense_tail(xp, xn, [y.reshape(bp * tp, w) for y in (y_a[0], y_b, y_c, y_d)], P, l,
                         _tile(bp * tp, 1024))
        for k in names:
            res_p[k].append(nst[k])

        proj, pif, xn = _in_proj(xs, P['norm_pre'], P['w_in'], P['w_if'], l, bs * ts)
        proj3 = proj.reshape(bs, ts, n_main)
        projc = pad_t(proj3, CHUNK)
        (y_a, vn), y_b, y_d, nst = _branches(projc, pad_t(pif.reshape(bs, ts, LANE), CHUNK), P, l, st_s, l, ts,
                                             past_len == 0, True)
        y_c = _sb_decode(page_table, sb_bias, pad_t(proj3[:, :, 5 * w:6 * w], DEC_ROWS),
                         projc[:, :, K_GROUP * w:(K_GROUP + 1) * w], projc[:, :, V_GROUP * w:(V_GROUP + 1) * w],
                         pad_t(proj3[:, :, 8 * w:9 * w], DEC_ROWS), ck, cv, l, pages_per_step)
        xs = _dense_tail(xs, xn, [y[:, :ts].reshape(bs * ts, w) for y in (y_a, y_b, y_c, y_d)], P, l, bs * ts)
        for k in names:
            res_s[k].append(nst[k])
        ks_l.append(proj3[:, :, K_GROUP * w:(K_GROUP + 1) * w].reshape(bs, ts, HEADS, hd))
        vs_l.append(proj3[:, :, V_GROUP * w:(V_GROUP + 1) * w].reshape(bs, ts, HEADS, hd))
        gv_l.append(vn[:, :ts])

    st = lambda lst: jnp.stack(lst, axis=0)
    k_p, v_p = (a.reshape(depth, bp, tp, HEADS, hd) for a in kv_p)
    return (xp.reshape(bp, tp, d), xs.reshape(bs, ts, d), k_p, v_p, st(ks_l), st(vs_l),
            st(res_p['conv_b']), st(res_p['h_b']), st(res_s['conv_b']), st(res_s['h_b']),
            st(res_p['conv_d']), st(res_p['C']), st(res_p['n']), st(res_p['m']),
            st(res_s['conv_d']), st(res_s['C']), st(res_s['n']), st(res_s['m']),
            st(gv_l))
```

```python
import functools

import jax
import jax.numpy as jnp
from jax import lax
from jax.experimental import pallas as pl
from jax.experimental.pallas import tpu as pltpu

F32 = jnp.float32
BF16 = jnp.bfloat16
SDS = jax.ShapeDtypeStruct

EPS = 1e-6
N_BRANCH = 4
HEADS = 4
CONV_W = 4
CHUNK = 128
LRU_C = 8.0
LANE = 128
SUBLANE = 8
CONV_PAD = SUBLANE
DEC_ROWS = 16
K_GROUP, V_GROUP = 6, 7
VMEM_LIMIT = 56 * 1024 * 1024


def _log_sigmoid(z):
    return jnp.minimum(z, 0.0) - jnp.log1p(jnp.exp(-jnp.abs(z)))


def _silu(x):
    return x * jax.nn.sigmoid(x)


def _dot(a, b):
    return jnp.dot(a, b, preferred_element_type=F32)


def _dot_nt(a, b):
    return lax.dot_general(a, b, (((1,), (1,)), ((), ())), preferred_element_type=F32)


def _split(x):
    hi = x.astype(BF16)
    return hi, (x - hi.astype(F32)).astype(BF16)


def _cumsum_dot_right(x, m):
    hi, lo = _split(x)
    return _dot(hi, m) + _dot(lo, m)


def _cumsum_dot_left(m, x):
    hi, lo = _split(x)
    return _dot(m, hi) + _dot(m, lo)


def _params(*sem):
    return pltpu.CompilerParams(dimension_semantics=sem, vmem_limit_bytes=VMEM_LIMIT)


def _layer_spec(a, l):
    return pl.BlockSpec((None,) + a.shape[1:], lambda *_: (l,) + (0,) * (a.ndim - 1))


def _in_proj_kernel(x_ref, g_ref, w_ref, wif_ref, *rest, kv_out):
    if kv_out:
        proj_ref, pif_ref, xn_ref, k_ref, v_ref = rest[-5:]
    else:
        proj_ref, pif_ref, xn_ref = rest
    j = pl.program_id(1)

    @pl.when(j == 0)
    def _():
        x = x_ref[...]
        ms = jnp.mean(x * x, axis=-1, keepdims=True)
        xn = (x * lax.rsqrt(ms + EPS) * g_ref[...]).astype(BF16)
        xn_ref[...] = xn
        pif_ref[...] = _dot(xn, wif_ref[...])

    proj = _dot(xn_ref[...], w_ref[...])
    proj_ref[...] = proj
    if kv_out:
        tm = proj.shape[0]
        for grp, ref in ((K_GROUP, k_ref), (V_GROUP, v_ref)):
            @pl.when(j == grp)
            def _(ref=ref):
                for h in range(HEADS):
                    ref[pl.ds(h, tm, stride=HEADS), :] = proj[:, h * LANE:(h + 1) * LANE]


def _in_proj(x, g, w_in, w_if, l, tm, kv_prev=None, kv_out=False):
    rows, d = x.shape
    depth = w_in.shape[0]
    tn = d // N_BRANCH
    n = 12 * tn
    n_i, n_j = rows // tm, n // tn
    in_specs = [pl.BlockSpec((tm, d), lambda i, j: (jnp.minimum(i + (j >= n_j // 2), n_i - 1), 0)),
                _layer_spec(g, l),
                pl.BlockSpec((None, d, tn), lambda i, j: (l, 0, j)),
                _layer_spec(w_if, l)]
    out_specs = [pl.BlockSpec((tm, tn), lambda i, j: (i, j)),
                 pl.BlockSpec((tm, LANE), lambda i, j: (i, 0)),
                 pl.BlockSpec((tm, d), lambda i, j: (i, 0))]
    out_shape = [SDS((rows, n), F32), SDS((rows, LANE), F32), SDS((rows, d), BF16)]
    args = [x, g, w_in, w_if]
    aliases = {}
    if kv_out:
        kv_spec = pl.BlockSpec((None, tm * HEADS, LANE), lambda i, j: (l, i, 0))
        out_specs += [kv_spec, kv_spec]
        out_shape += [SDS((depth, rows * HEADS, LANE), F32)] * 2
        if kv_prev is not None:
            in_specs += [pl.BlockSpec(memory_space=pl.ANY)] * 2
            args += list(kv_prev)
            aliases = {4: 3, 5: 4}
    return pl.pallas_call(
        functools.partial(_in_proj_kernel, kv_out=kv_out),
        grid=(rows // tm, n // tn),
        in_specs=in_specs,
        out_specs=out_specs,
        out_shape=out_shape,
        input_output_aliases=aliases,
        compiler_params=_params("parallel", "arbitrary"),
        name="in_proj",
    )(*args)


def _gmlp_kernel(u_ref, v_ref, z_ref, lg_ref, lb_ref, ws_ref, bst_ref, y_ref, *vn_out, n_chunks):
    v = v_ref[...]
    vc = v - jnp.mean(v, axis=-1, keepdims=True)
    var = jnp.mean(vc * vc, axis=-1, keepdims=True)
    vn = vc * lax.rsqrt(var + EPS) * lg_ref[...] + lb_ref[...]
    if vn_out:
        vn_out[0][...] = vn
    vnb = vn.astype(BF16)
    row = lax.broadcasted_iota(jnp.int32, (CHUNK, CHUNK), 0)
    col = lax.broadcasted_iota(jnp.int32, (CHUNK, CHUNK), 1)
    for g in range(HEADS):
        gs = slice(g * LANE, (g + 1) * LANE)
        wm = jnp.where(row >= col, ws_ref[g], 0.0).astype(BF16)
        bcol = bst_ref[:, g:g + 1]
        for c in range(n_chunks):
            ts = slice(c * CHUNK, (c + 1) * CHUNK)
            s = _dot(wm, vnb[ts, gs]) + bcol
            y_ref[ts, gs] = (u_ref[ts, gs] * s * _silu(z_ref[ts, gs])).astype(BF16)


def _gmlp(proj3, P, l, tt, emit_vn):
    b, t, _ = proj3.shape
    w = P['ln_g'].shape[-1]
    col = lambda c: pl.BlockSpec((None, tt, w), lambda i, j: (i, j, c))
    out_specs = [pl.BlockSpec((None, tt, w), lambda i, j: (i, j, 0))]
    out_shape = [SDS((b, t, w), BF16)]
    if emit_vn:
        out_specs.append(pl.BlockSpec((None, tt, w), lambda i, j: (i, j, 0)))
        out_shape.append(SDS((b, t, w), F32))
    params = [P[k] for k in ('ln_g', 'ln_b', 'ws', 'bs_t')]
    return pl.pallas_call(
        functools.partial(_gmlp_kernel, n_chunks=tt // CHUNK),
        grid=(b, t // tt),
        in_specs=[col(0), col(1), col(2)] + [_layer_spec(a, l) for a in params],
        out_specs=out_specs,
        out_shape=out_shape,
        compiler_params=_params("parallel", "parallel"),
        name="gmlp",
    )(proj3, proj3, proj3, *params)


def _causal_conv(buf_ref, x, cw_ref, cb_ref, tt):
    h0 = CONV_PAD - (CONV_W - 1)
    buf_ref[CONV_PAD:CONV_PAD + tt, :] = x
    y = cb_ref[...] + cw_ref[0:1, :] * buf_ref[h0:h0 + tt, :]
    for j in range(1, CONV_W):
        y = y + cw_ref[j:j + 1, :] * buf_ref[h0 + j:h0 + j + tt, :]
    buf_ref[h0:CONV_PAD, :] = x[tt - (CONV_W - 1):tt, :]
    return y


def _state_spec(a, sl):
    return pl.BlockSpec((None, None) + a.shape[2:], lambda i, j: (sl, i) + (0,) * (a.ndim - 2))


def _rglru_kernel(x_ref, z_ref, hist_ref, h0_ref, cw_ref, cb_ref, wa_ref, ba_ref, wx_ref, bx_ref, lam_ref,
                  y_ref, hist_out_ref, hlast_ref, buf_ref, hcar_ref, *, tt, t_valid, first_pos_zero):
    t = pl.program_id(1)
    h0 = CONV_PAD - (CONV_W - 1)

    @pl.when(t == 0)
    def _():
        buf_ref[h0:CONV_PAD, :] = hist_ref[...]
        hcar_ref[...] = h0_ref[...]

    x = x_ref[...]
    xc = _causal_conv(buf_ref, x, cw_ref, cb_ref, tt)
    xcb = xc.astype(BF16)
    ra, rx = [], []
    for blk in range(HEADS):
        bs = slice(blk * LANE, (blk + 1) * LANE)
        ra.append(_dot(xcb[:, bs], wa_ref[blk]))
        rx.append(_dot(xcb[:, bs], wx_ref[blk]))
    r = jax.nn.sigmoid(jnp.concatenate(ra, axis=-1) + ba_ref[...])
    i = jax.nn.sigmoid(jnp.concatenate(rx, axis=-1) + bx_ref[...])
    log_a = LRU_C * r * _log_sigmoid(lam_ref[...])
    a = jnp.exp(log_a)
    mult = jnp.sqrt(-jnp.tanh(log_a) * (a * a + 1.0))
    rowi = lax.broadcasted_iota(jnp.int32, (tt, 1), 0)
    if first_pos_zero:
        mult = jnp.where(rowi + t * tt == 0, 1.0, mult)
    u = mult * (i * xc)
    d = 1
    while d < tt:
        if d % SUBLANE == 0:
            u = jnp.concatenate([u[:d], a[d:] * u[:tt - d] + u[d:]], axis=0)
            a = jnp.concatenate([a[:d], a[d:] * a[:tt - d]], axis=0)
        else:
            keep = rowi >= d
            u = a * jnp.where(keep, pltpu.roll(u, d, axis=0), 0.0) + u
            a = a * jnp.where(keep, pltpu.roll(a, d, axis=0), 1.0)
        d *= 2
    h = a * hcar_ref[...] + u
    hcar_ref[...] = h[tt - 1:tt, :]
    y_ref[...] = (h * _silu(z_ref[...])).astype(BF16)

    @pl.when(t == pl.num_programs(1) - 1)
    def _():
        hist_out_ref[...] = x[t_valid - (CONV_W - 1):t_valid, :]
        hlast_ref[...] = h[t_valid - 1:t_valid, :]


def _rglru(proj3, st, sl, P, l, tt, t_valid, first_pos_zero):
    b, t, _ = proj3.shape
    w = P['lru_cb'].shape[-1]
    assert t_valid >= CONV_W - 1 and (t_valid == tt or t == tt)
    col = lambda c: pl.BlockSpec((None, tt, w), lambda i, j: (i, j, c))
    per_b = lambda r: pl.BlockSpec((None, r, w), lambda i, j: (i, 0, 0))
    params = [P[k] for k in ('lru_cw', 'lru_cb', 'lru_wa', 'lru_ba', 'lru_wx', 'lru_bx', 'lru_lam')]
    return pl.pallas_call(
        functools.partial(_rglru_kernel, tt=tt, t_valid=t_valid, first_pos_zero=first_pos_zero),
        grid=(b, t // tt),
        in_specs=[col(3), col(4), _state_spec(st['conv_b'], sl), _state_spec(st['h_b'], sl)]
                 + [_layer_spec(a, l) for a in params],
        out_specs=[pl.BlockSpec((None, tt, w), lambda i, j: (i, j, 0)), per_b(CONV_W - 1), per_b(1)],
        out_shape=[SDS((b, t, w), BF16), SDS((b, CONV_W - 1, w), F32), SDS((b, 1, w), F32)],
        scratch_shapes=[pltpu.VMEM((CONV_PAD + tt, w), F32), pltpu.VMEM((1, w), F32)],
        compiler_params=_params("parallel", "arbitrary"),
        name="rglru",
    )(proj3, proj3, st['conv_b'], st['h_b'], *params)


def _later_matrix(n):
    row = lax.broadcasted_iota(jnp.int32, (n, n), 0)
    col = lax.broadcasted_iota(jnp.int32, (n, n), 1)
    return (row > col).astype(BF16)


def _sb_logits(z):
    log_b = jnp.minimum(z, 0.0) - jnp.log(1.0 + jnp.exp(-jnp.abs(z)))
    return log_b, log_b - z


def _sb_span(qb, kspan, vspan, r, later, bias, mask):
    cb = later.shape[0]
    log_b, log_1mb = _sb_logits(_dot_nt(qb, kspan) + bias)
    if mask is not None:
        log_1mb = jnp.where(mask, log_1mb, 0.0)
    l1b = log_1mb.astype(BF16)
    blocks = [slice(j * cb, (j + 1) * cb) for j in range(kspan.shape[0] // cb)]
    suffix = [_dot(l1b[:, js], later) for js in blocks]
    atts = []
    for js, suf in zip(reversed(blocks), reversed(suffix)):
        att = jnp.exp(log_b[:, js] + suf + r)
        if mask is not None:
            att = jnp.where(mask[:, js], att, 0.0)
        atts.insert(0, att.astype(BF16))
        r = r + jnp.sum(log_1mb[:, js], axis=-1, keepdims=True)
    return _dot(jnp.concatenate(atts, axis=-1), vspan), r


def _sb_attn_kernel(bias_ref, q_ref, k_ref, v_ref, z_ref, y_ref, kb_ref, vb_ref, acc_ref, r_ref, *,
                    layer, tq, cb, scale):
    qi = pl.program_id(2)

    @pl.when(qi == 0)
    def _():
        kb_ref[...] = k_ref[...].astype(BF16)
        vb_ref[...] = v_ref[...].astype(BF16)

    bias = bias_ref[layer, pl.program_id(1)]
    later = _later_matrix(cb)
    qb = (q_ref[...] * scale).astype(BF16)

    def span(k0, n, r, mask):
        return _sb_span(qb, kb_ref[pl.ds(k0, n), :], vb_ref[pl.ds(k0, n), :], r, later, bias, mask)

    rr = lax.broadcasted_iota(jnp.int32, (tq, tq), 0)
    cc = lax.broadcasted_iota(jnp.int32, (tq, tq), 1)
    acc_ref[...], r_ref[...] = span(pl.multiple_of(qi * tq, tq), tq, jnp.zeros((tq, 1), F32), cc < rr)

    def body(i, carry):
        pv, r_ref[...] = span(pl.multiple_of((qi - 2 - 2 * i) * tq, tq), 2 * tq, r_ref[...], None)
        acc_ref[...] += pv
        return carry

    lax.fori_loop(0, qi // 2, body, 0)

    @pl.when(qi % 2 == 1)
    def _():
        pv, r_ref[...] = span(0, tq, r_ref[...], None)
        acc_ref[...] += pv

    y_ref[...] = (acc_ref[...] * _silu(z_ref[...])).astype(BF16)


def _sb_attn(proj3, bias, l, tq):
    b, t, _ = proj3.shape
    hd = LANE
    cb = min(tq, 2 * CHUNK)
    assert tq % cb == 0
    qcol, kcol, vcol, zcol = (5 * HEADS, K_GROUP * HEADS, V_GROUP * HEADS, 8 * HEADS)
    tile = lambda c0: pl.BlockSpec((None, tq, hd), lambda i, h, j: (i, j, c0 + h))
    whole = lambda c0: pl.BlockSpec((None, t, hd), lambda i, h, j: (i, 0, c0 + h))
    return pl.pallas_call(
        functools.partial(_sb_attn_kernel, layer=l, tq=tq, cb=cb, scale=hd ** -0.5),
        grid=(b, HEADS, t // tq),
        in_specs=[pl.BlockSpec(memory_space=pltpu.SMEM), tile(qcol), whole(kcol), whole(vcol), tile(zcol)],
        out_specs=pl.BlockSpec((None, tq, hd), lambda i, h, j: (i, j, h)),
        out_shape=SDS((b, t, HEADS * hd), BF16),
        scratch_shapes=[pltpu.VMEM((t, hd), BF16), pltpu.VMEM((t, hd), BF16),
                        pltpu.VMEM((tq, hd), F32), pltpu.VMEM((tq, 1), F32)],
        compiler_params=_params("parallel", "parallel", "arbitrary"),
        name="sb_attn",
    )(bias, proj3, proj3, proj3, proj3)


def _sb_decode_kernel(pt_ref, bias_ref, q_ref, kn_ref, vn_ref, z_ref, *rest, layer, n_pages, scale):
    k_pages, v_pages = rest[:n_pages], rest[n_pages:2 * n_pages]
    y_ref, acc_ref, r_ref = rest[2 * n_pages:]
    s = pl.program_id(1)
    later = _later_matrix(CHUNK)

    def update(keys, values, n_blk, masked):
        zs = []
        for h in range(HEADS):
            kh = jnp.concatenate([keys(h, g) for g in range(n_blk)], axis=0).astype(BF16)
            zh = _dot_nt(q_ref[:, h * LANE:(h + 1) * LANE].astype(BF16), kh) * scale + bias_ref[layer, h]
            zs += [zh[:, g * CHUNK:(g + 1) * CHUNK] for g in range(n_blk)]
        log_b, log_1mb = _sb_logits(jnp.concatenate(zs, axis=0))
        if masked:
            rr = lax.broadcasted_iota(jnp.int32, log_b.shape, 0) & (DEC_ROWS - 1)
            mask = lax.broadcasted_iota(jnp.int32, log_b.shape, 1) < rr
            log_1mb = jnp.where(mask, log_1mb, 0.0)
        suffix = _dot(log_1mb.astype(BF16), later)
        total = jnp.sum(log_1mb, axis=-1, keepdims=True)
        for h in range(HEADS):
            run = r_ref[h]
            atts = []
            for g in range(n_blk):
                rows = slice((h * n_blk + g) * DEC_ROWS, (h * n_blk + g + 1) * DEC_ROWS)
                att = jnp.exp(log_b[rows] + suffix[rows] + run)
                if masked:
                    att = jnp.where(mask[rows], att, 0.0)
                atts.append(att.astype(BF16))
                run = run + total[rows]
            vh = jnp.concatenate([values(h, g) for g in range(n_blk)], axis=0).astype(BF16)
            acc_ref[h] += _dot(jnp.concatenate(atts, axis=-1), vh)
            r_ref[h] = run

    @pl.when(s == 0)
    def _():
        acc_ref[...] = jnp.zeros_like(acc_ref)
        r_ref[...] = jnp.zeros_like(r_ref)
        update(lambda h, g: kn_ref[:, h * LANE:(h + 1) * LANE], lambda h, g: vn_ref[:, h * LANE:(h + 1) * LANE],
               1, True)

    update(lambda h, g: k_pages[g][pl.ds(h, CHUNK, stride=HEADS), :],
           lambda h, g: v_pages[g][pl.ds(h, CHUNK, stride=HEADS), :], n_pages, False)

    @pl.when(s == pl.num_programs(1) - 1)
    def _():
        acc = jnp.concatenate([acc_ref[h] for h in range(HEADS)], axis=-1)
        y_ref[...] = (acc * _silu(z_ref[...])).astype(BF16)


def _sb_decode(page_table, bias, q, k_new, v_new, z, cache_k, cache_v, layer, pages_per_step):
    b, n_tab = page_table.shape
    w = q.shape[-1]
    g = pages_per_step
    assert n_tab % g == 0
    page_rows, hd = cache_k.shape[2:]
    per_b = lambda r: pl.BlockSpec((None, r, w), lambda i, s, pt, bs: (i, 0, 0))

    def page_spec(j):
        return pl.BlockSpec((None, None, page_rows, hd),
                            lambda i, s, pt, bs: (layer, pt[i, n_tab - 1 - (s * g + j)], 0, 0))

    grid_spec = pltpu.PrefetchScalarGridSpec(
        num_scalar_prefetch=2,
        grid=(b, n_tab // g),
        in_specs=[per_b(DEC_ROWS), per_b(CHUNK), per_b(CHUNK), per_b(DEC_ROWS)]
                 + [page_spec(j) for j in range(g)] * 2,
        out_specs=per_b(DEC_ROWS),
        scratch_shapes=[pltpu.VMEM((HEADS, DEC_ROWS, hd), F32), pltpu.VMEM((HEADS, DEC_ROWS, hd), F32)],
    )
    return pl.pallas_call(
        functools.partial(_sb_decode_kernel, layer=layer, n_pages=g, scale=hd ** -0.5),
        grid_spec=grid_spec,
        out_shape=SDS((b, DEC_ROWS, w), BF16),
        compiler_params=_params("parallel", "arbitrary"),
        name="sb_decode",
    )(page_table, bias, q, k_new, v_new, z, *([cache_k] * g), *([cache_v] * g))


def _mlstm_kernel(x_ref, z_ref, o_ref, gc_ref, gr_ref, hist_ref, c0_ref, n0_ref, m0_ref,
                  cw_ref, cb_ref, wq_ref, wk_ref, wv_ref, brow_ref, bcol_ref, ng_ref, sk_ref,
                  y_ref, hist_out_ref, c_out_ref, n_out_ref, m_out_ref,
                  buf_ref, c_scr, n_scr, m_scr, *, t_valid, scale, n_chunks):
    step = pl.program_id(1)
    L = CHUNK
    rows = n_chunks * L
    h0 = CONV_PAD - (CONV_W - 1)

    @pl.when(step == 0)
    def _():
        buf_ref[h0:CONV_PAD, :] = hist_ref[...]
        c_scr[...] = c0_ref[...]
        n_scr[...] = n0_ref[...]
        m_scr[...] = m0_ref[...]

    x = x_ref[...]
    xc = _silu(_causal_conv(buf_ref, x, cw_ref, cb_ref, rows))
    xcb = xc.astype(BF16)
    xb = x.astype(BF16)
    gc = gc_ref[...] + brow_ref[...]
    gr = gr_ref[...] + bcol_ref[...]
    row = lax.broadcasted_iota(jnp.int32, (L, L), 0)
    col = lax.broadcasted_iota(jnp.int32, (L, L), 1)
    causal = row >= col
    tril, triu = causal.astype(BF16), (row <= col).astype(BF16)
    lsc, lsr = _log_sigmoid(gc), _log_sigmoid(gr)
    rowi = lax.broadcasted_iota(jnp.int32, (L, 1), 0)
    H = range(HEADS)
    CH = [(ci, h) for ci in range(n_chunks) for h in H]
    rs = [slice(ci * L, (ci + 1) * L) for ci in range(n_chunks)]
    hs = [slice(h * LANE, (h + 1) * LANE) for h in H]
    n_all, m_all = n_scr[...], m_scr[...]
    bc_all = [_cumsum_dot_left(tril, lsc[rs[ci], :]) for ci in range(n_chunks)]
    br_all = [_cumsum_dot_right(lsr[:, rs[ci]], triu) for ci in range(n_chunks)]
    q = {p: _dot(xcb[rs[p[0]], hs[p[1]]], wq_ref[p[1]]) for p in CH}
    k = {p: _dot(xcb[rs[p[0]], hs[p[1]]], wk_ref[p[1]]) * scale for p in CH}
    vb = {p: _dot(xb[rs[p[0]], hs[p[1]]], wv_ref[p[1]]).astype(BF16) for p in CH}
    qb = {p: q[p].astype(BF16) for p in CH}
    qk = {p: _dot_nt(qb[p], k[p].astype(BF16)) for p in CH}
    ig_c = {(ci, h): gc[rs[ci], h:h + 1] for ci, h in CH}
    b_c = {(ci, h): bc_all[ci][:, HEADS + h:HEADS + h + 1] for ci, h in CH}
    dmat = {(ci, h): jnp.where(causal, b_c[ci, h] - br_all[ci][HEADS + h:HEADS + h + 1, :] + gr[h:h + 1, rs[ci]],
                               -jnp.inf) for ci, h in CH}
    dmax = {p: jnp.max(dmat[p], axis=-1, keepdims=True) for p in CH}
    m_h, m_new, decay, kw = {}, {}, {}, {}
    for ci, h in CH:
        m_h[ci, h] = m_all[h:h + 1, 0:1] if ci == 0 else m_new[ci - 1, h]
        b_last = b_c[ci, h][t_valid - 1:t_valid, :]
        g = b_last - b_c[ci, h] + ig_c[ci, h]
        if t_valid < L:
            g = jnp.where(rowi < t_valid, g, -jnp.inf)
        m_new[ci, h] = jnp.maximum(b_last + m_h[ci, h], jnp.max(g, axis=0, keepdims=True))
        kw[ci, h] = k[ci, h] * jnp.exp(g - m_new[ci, h])
        decay[ci, h] = jnp.exp(b_last + m_h[ci, h] - m_new[ci, h])
    inter = {p: b_c[p] + m_h[p] for p in CH}
    m_t = {p: jnp.maximum(dmax[p], inter[p]) for p in CH}
    s = {p: qk[p] * jnp.exp(dmat[p] - m_t[p]) for p in CH}
    w_inter = {p: jnp.exp(inter[p] - m_t[p]) for p in CH}
    sv = {p: _dot(s[p].astype(BF16), vb[p]) for p in CH}
    kv = {p: _dot(kw[p].T.astype(BF16), vb[p]) for p in CH}
    qc, qn = {}, {}
    c_cur = [c_scr[h] for h in H]
    n_cur = [n_all[h:h + 1, :] for h in H]
    for ci, h in CH:
        qc[ci, h] = _dot(qb[ci, h], c_cur[h].astype(BF16))
        qn[ci, h] = jnp.sum(q[ci, h] * n_cur[h], axis=-1, keepdims=True)
        c_cur[h] = decay[ci, h] * c_cur[h] + kv[ci, h]
        n_cur[h] = decay[ci, h] * n_cur[h] + jnp.sum(kw[ci, h], axis=0, keepdims=True)
    for h in H:
        c_scr[h] = c_cur[h]
    n_scr[...] = jnp.concatenate(n_cur, axis=0)
    m_scr[...] = jnp.concatenate([jnp.broadcast_to(m_new[n_chunks - 1, h], (1, LANE)) for h in H], axis=0)
    num = {p: sv[p] + w_inter[p] * qc[p] for p in CH}
    den = {p: jnp.sum(s[p], axis=-1, keepdims=True) + w_inter[p] * qn[p] for p in CH}
    hh = {p: num[p] / jnp.maximum(jnp.abs(den[p]), jnp.exp(-m_t[p])) for p in CH}
    hh = {(ci, h): jax.nn.sigmoid(o_ref[rs[ci], hs[h]]) * hh[ci, h] for ci, h in CH}
    hc = {p: hh[p] - jnp.mean(hh[p], axis=-1, keepdims=True) for p in CH}
    outs = {p: hc[p] * lax.rsqrt(jnp.mean(hc[p] * hc[p], axis=-1, keepdims=True) + EPS) for p in CH}
    hn = jnp.concatenate([jnp.concatenate([outs[ci, h] for h in H], axis=-1) for ci in range(n_chunks)], axis=0)
    hn = hn * ng_ref[...] + sk_ref[...] * xc
    y_ref[...] = (hn * _silu(z_ref[...])).astype(BF16)

    @pl.when(step == pl.num_programs(1) - 1)
    def _():
        last = (n_chunks - 1) * L + t_valid
        hist_out_ref[...] = x[last - (CONV_W - 1):last, :]
        c_out_ref[...] = c_scr[...]
        n_out_ref[...] = n_scr[...]
        m_out_ref[...] = m_scr[...]


def _mlstm(proj3, gates_col, gates_row, st, sl, P, l, t_valid):
    b, t, _ = proj3.shape
    w = P['ml_cb'].shape[-1]
    hd = w // HEADS
    L = CHUNK
    assert t_valid >= CONV_W - 1 and (t_valid == L or t == L)
    n_chunks = 8 if t % (8 * L) == 0 else 1
    rows = n_chunks * L
    col = lambda c: pl.BlockSpec((None, rows, w), lambda i, j: (i, j, c))
    per_b = lambda *s: pl.BlockSpec((None,) + s, lambda i, j: (i,) + (0,) * len(s))
    states = [st[k] for k in ('conv_d', 'C', 'n', 'm')]
    params = [P[k] for k in ('ml_cw', 'ml_cb', 'ml_wq', 'ml_wk', 'ml_wv', 'ml_brow', 'ml_bcol', 'ml_ng', 'ml_sk')]
    return pl.pallas_call(
        functools.partial(_mlstm_kernel, t_valid=t_valid, scale=hd ** -0.5, n_chunks=n_chunks),
        grid=(b, t // rows),
        in_specs=[col(9), col(10), col(11),
                  pl.BlockSpec((None, rows, LANE), lambda i, j: (i, j, 0)),
                  pl.BlockSpec((None, 2 * HEADS, rows), lambda i, j: (i, 0, j))]
                 + [_state_spec(a, sl) for a in states] + [_layer_spec(a, l) for a in params],
        out_specs=[pl.BlockSpec((None, rows, w), lambda i, j: (i, j, 0)),
                   per_b(CONV_W - 1, w), per_b(HEADS, hd, hd), per_b(HEADS, hd), per_b(HEADS, LANE)],
        out_shape=[SDS((b, t, w), BF16), SDS((b, CONV_W - 1, w), F32), SDS((b, HEADS, hd, hd), F32),
                   SDS((b, HEADS, hd), F32), SDS((b, HEADS, LANE), F32)],
        scratch_shapes=[pltpu.VMEM((CONV_PAD + rows, w), F32), pltpu.VMEM((HEADS, hd, hd), F32),
                        pltpu.VMEM((HEADS, hd), F32), pltpu.VMEM((HEADS, LANE), F32)],
        compiler_params=_params("parallel", "arbitrary"),
        name="mlstm",
    )(proj3, proj3, proj3, gates_col, gates_row, *states, *params)


def _merge_kernel(xn_ref, a_ref, b_ref, c_ref, d_ref, wg0_ref, wg1_ref, wg2_ref, wg3_ref, bg_ref, wb_ref, o_ref):
    xn = xn_ref[...]
    acc = None
    branches = (a_ref, b_ref, c_ref, d_ref)
    gates = (wg0_ref, wg1_ref, wg2_ref, wg3_ref)
    for m in range(N_BRANCH):
        gate = jax.nn.sigmoid(_dot(xn, gates[m][...]) + bg_ref[m:m + 1, :])
        term = gate * _dot(branches[m][...], wb_ref[m])
        acc = term if acc is None else acc + term
    o_ref[...] = acc.astype(BF16)


def _merge(xn, branches, P, l, tm, tn):
    rows, d = xn.shape
    w = branches[0].shape[-1]
    nj = d // tn
    gate_spec = lambda m: pl.BlockSpec((None, d, tn), lambda i, j: (l, 0, m * nj + j))
    return pl.pallas_call(
        _merge_kernel,
        grid=(rows // tm, nj),
        in_specs=[pl.BlockSpec((tm, d), lambda i, j: (i, 0))]
                 + [pl.BlockSpec((tm, w), lambda i, j: (i, 0))] * N_BRANCH
                 + [gate_spec(m) for m in range(N_BRANCH)]
                 + [pl.BlockSpec((None, N_BRANCH, tn), lambda i, j: (l, 0, j)),
                    pl.BlockSpec((None, N_BRANCH, w, tn), lambda i, j: (l, 0, 0, j))],
        out_specs=pl.BlockSpec((tm, tn), lambda i, j: (i, j)),
        out_shape=SDS((rows, d), BF16),
        compiler_params=_params("parallel", "arbitrary"),
        name="merge",
    )(xn, *branches, *([P['w_gate']] * N_BRANCH), P['b_gate'], P['w_branch'])


def _out_proj_kernel(m_ref, x_ref, w_ref, g_ref, o_ref):
    out = _dot(m_ref[...], w_ref[...])
    ms = jnp.mean(out * out, axis=-1, keepdims=True)
    o_ref[...] = x_ref[...] + out * lax.rsqrt(ms + EPS) * g_ref[...]


def _out_proj(merged, x, P, l, tm):
    rows, d = x.shape
    return pl.pallas_call(
        _out_proj_kernel,
        grid=(rows // tm,),
        in_specs=[pl.BlockSpec((tm, d), lambda i: (i, 0)),
                  pl.BlockSpec((tm, d), lambda i: (i, 0)),
                  _layer_spec(P['w_out'], l),
                  _layer_spec(P['norm_post'], l)],
        out_specs=pl.BlockSpec((tm, d), lambda i: (i, 0)),
        out_shape=SDS((rows, d), F32),
        compiler_params=_params("parallel"),
        name="out_proj",
    )(merged, x, P['w_out'], P['norm_post'])


def _tile(n, pref):
    return pref if n % pref == 0 else n


def _branches(proj3, gates_col, P, l, st, sl, t_valid, first_pos_zero, emit_vn):
    b, t, _ = proj3.shape
    gm = _gmlp(proj3, P, l, _tile(t, 8 * CHUNK), emit_vn)
    tt = _tile(t, 4 * CHUNK) if t_valid == CHUNK else t
    y_b, hist_b, h_b = _rglru(proj3, st, sl, P, l, tt, tt if t_valid == CHUNK else t_valid, first_pos_zero)
    gates_row = jnp.swapaxes(gates_col[:, :, :2 * HEADS], 1, 2)
    y_d, hist_d, c, n, m = _mlstm(proj3, gates_col, gates_row, st, sl, P, l, t_valid)
    new_st = dict(conv_b=hist_b, h_b=h_b[:, 0], conv_d=hist_d, C=c, n=n, m=m[:, :, 0])
    return gm, y_b, y_d, new_st


def _dense_tail(x, xn, branches, P, l, tm):
    merged = _merge(xn, branches, P, l, tm, _tile(x.shape[1], 512))
    return _out_proj(merged, x, P, l, _tile(x.shape[0], 512))


def _state_in(conv_b, h_b, conv_d, c, n, m):
    return dict(conv_b=conv_b, h_b=h_b[:, :, None, :], conv_d=conv_d, C=c, n=n,
                m=jnp.broadcast_to(m[..., None], m.shape + (LANE,)))


def kernel(x_prompt, x_sample, cache_k, cache_v, page_table, state_rglru_conv, state_rglru_h, state_mlstm_conv, state_mlstm_c, state_mlstm_n, state_mlstm_m, norm_pre, norm_post, w_in, gmlp_ln_g, gmlp_ln_b, gmlp_ws, gmlp_bs, lru_conv_w, lru_conv_b, lru_wa, lru_ba, lru_wx, lru_bx, lru_lambda, ml_conv_w, ml_conv_b, ml_wq, ml_wk, ml_wv, ml_bi, ml_bf, ml_norm_g, ml_skip, sb_bias, w_branch, w_gate, b_gate, w_out):
    bp, tp, d = x_prompt.shape
    bs, ts, _ = x_sample.shape
    depth = w_in.shape[0]
    w = d // N_BRANCH
    hd = w // HEADS
    n_main = 12 * w
    page = cache_k.shape[2]
    past_len = page_table.shape[1] * page
    assert tp % CHUNK == 0 and page == CHUNK and ts <= DEC_ROWS and cache_k.shape[3:] == (HEADS, hd)
    ck = cache_k.reshape(depth, cache_k.shape[1], page * HEADS, hd)
    cv = cache_v.reshape(depth, cache_v.shape[1], page * HEADS, hd)
    n_tab = page_table.shape[1]
    pages_per_step = next(g for g in (16, 8, 4, 2, 1) if n_tab % g == 0)

    row3 = lambda a: a[:, None, :]
    gate_bias = jnp.concatenate([ml_bi, ml_bf], axis=-1)
    P = dict(
        norm_pre=row3(norm_pre), norm_post=row3(norm_post),
        w_in=w_in.astype(BF16),
        w_if=jnp.pad(w_in[:, :, n_main:], ((0, 0), (0, 0), (0, LANE - 2 * HEADS))).astype(BF16),
        ln_g=row3(gmlp_ln_g), ln_b=row3(gmlp_ln_b), ws=gmlp_ws, bs_t=jnp.swapaxes(gmlp_bs, 1, 2),
        lru_cw=lru_conv_w, lru_cb=row3(lru_conv_b), lru_wa=lru_wa.astype(BF16), lru_ba=row3(lru_ba),
        lru_wx=lru_wx.astype(BF16), lru_bx=row3(lru_bx), lru_lam=row3(lru_lambda),
        ml_cw=ml_conv_w, ml_cb=row3(ml_conv_b), ml_wq=ml_wq.astype(BF16), ml_wk=ml_wk.astype(BF16),
        ml_wv=ml_wv.astype(BF16), ml_brow=row3(jnp.pad(gate_bias, ((0, 0), (0, LANE - 2 * HEADS)))),
        ml_bcol=gate_bias[:, :, None], ml_ng=row3(ml_norm_g), ml_sk=row3(ml_skip),
        w_gate=w_gate.astype(BF16), b_gate=b_gate.reshape(depth, N_BRANCH, d),
        w_branch=w_branch.astype(BF16), w_out=w_out.astype(BF16))

    xp = x_prompt.reshape(bp * tp, d)
    xs = x_sample.reshape(bs * ts, d)
    st_p = _state_in(jnp.zeros((1, bp, CONV_W - 1, w), F32), jnp.zeros((1, bp, w), F32),
                     jnp.zeros((1, bp, CONV_W - 1, w), F32), jnp.zeros((1, bp, HEADS, hd, hd), F32),
                     jnp.zeros((1, bp, HEADS, hd), F32), jnp.zeros((1, bp, HEADS), F32))
    st_s = _state_in(state_rglru_conv, state_rglru_h, state_mlstm_conv, state_mlstm_c, state_mlstm_n, state_mlstm_m)
    names = ('conv_b', 'h_b', 'conv_d', 'C', 'n', 'm')
    res_p = {k: [] for k in names}
    res_s = {k: [] for k in names}
    ks_l, vs_l, gv_l = [], [], []
    kv_p = [jnp.zeros((depth, bp * tp * HEADS, LANE), F32) for _ in range(2)]
    pad_t = lambda a, rows: jnp.pad(a, ((0, 0), (0, rows - ts), (0, 0)))

    for l in range(depth):
        proj, pif, xn, *kv_p = _in_proj(xp, P['norm_pre'], P['w_in'], P['w_if'], l, _tile(bp * tp, 1024),
                                        kv_prev=kv_p, kv_out=True)
        proj3 = proj.reshape(bp, tp, n_main)
        y_a, y_b, y_d, nst = _branches(proj3, pif.reshape(bp, tp, LANE), P, l, st_p, 0, CHUNK, True, False)
        y_c = _sb_attn(proj3, sb_bias, l, _tile(tp, 4 * CHUNK))
        xp = _dense_tail(xp, xn, [y.reshape(bp * tp, w) for y in (y_a[0], y_b, y_c, y_d)], P, l,
                         _tile(bp * tp, 1024))
        for k in names:
            res_p[k].append(nst[k])

        proj, pif, xn = _in_proj(xs, P['norm_pre'], P['w_in'], P['w_if'], l, bs * ts)
        proj3 = proj.reshape(bs, ts, n_main)
        projc = pad_t(proj3, CHUNK)
        (y_a, vn), y_b, y_d, nst = _branches(projc, pad_t(pif.reshape(bs, ts, LANE), CHUNK), P, l, st_s, l, ts,
                                             past_len == 0, True)
        y_c = _sb_decode(page_table, sb_bias, pad_t(proj3[:, :, 5 * w:6 * w], DEC_ROWS),
                         projc[:, :, K_GROUP * w:(K_GROUP + 1) * w], projc[:, :, V_GROUP * w:(V_GROUP + 1) * w],
                         pad_t(proj3[:, :, 8 * w:9 * w], DEC_ROWS), ck, cv, l, pages_per_step)
        xs = _dense_tail(xs, xn, [y[:, :ts].reshape(bs * ts, w) for y in (y_a, y_b, y_c, y_d)], P, l, bs * ts)
        for k in names:
            res_s[k].append(nst[k])
        ks_l.append(proj3[:, :, K_GROUP * w:(K_GROUP + 1) * w].reshape(bs, ts, HEADS, hd))
        vs_l.append(proj3[:, :, V_GROUP * w:(V_GROUP + 1) * w].reshape(bs, ts, HEADS, hd))
        gv_l.append(vn[:, :ts])

    st = lambda lst: jnp.stack(lst, axis=0)
    k_p, v_p = (a.reshape(depth, bp, tp, HEADS, hd) for a in kv_p)
    return (xp.reshape(bp, tp, d), xs.reshape(bs, ts, d), k_p, v_p, st(ks_l), st(vs_l),
            st(res_p['conv_b']), st(res_p['h_b']), st(res_s['conv_b']), st(res_s['h_b']),
            st(res_p['conv_d']), st(res_p['C']), st(res_p['n']), st(res_p['m']),
            st(res_s['conv_d']), st(res_s['C']), st(res_s['n']), st(res_s['m']),
            st(gv_l))
```

```python
import functools

import jax
import jax.numpy as jnp
from jax import lax
from jax.experimental import pallas as pl
from jax.experimental.pallas import tpu as pltpu

F32 = jnp.float32
BF16 = jnp.bfloat16
SDS = jax.ShapeDtypeStruct

EPS = 1e-6
N_BRANCH = 4
HEADS = 4
CONV_W = 4
CHUNK = 128
LRU_C = 8.0
LANE = 128
SUBLANE = 8
CONV_PAD = SUBLANE
DEC_ROWS = 16
K_GROUP, V_GROUP = 6, 7
VMEM_LIMIT = 56 * 1024 * 1024


def _log_sigmoid(z):
    return jnp.minimum(z, 0.0) - jnp.log1p(jnp.exp(-jnp.abs(z)))


def _silu(x):
    return x * jax.nn.sigmoid(x)


def _dot(a, b):
    return jnp.dot(a, b, preferred_element_type=F32)


def _dot_nt(a, b):
    return lax.dot_general(a, b, (((1,), (1,)), ((), ())), preferred_element_type=F32)


def _split(x):
    hi = x.astype(BF16)
    return hi, (x - hi.astype(F32)).astype(BF16)


def _cumsum_dot_right(x, m):
    hi, lo = _split(x)
    return _dot(hi, m) + _dot(lo, m)


def _cumsum_dot_left(m, x):
    hi, lo = _split(x)
    return _dot(m, hi) + _dot(m, lo)


def _params(*sem):
    return pltpu.CompilerParams(dimension_semantics=sem, vmem_limit_bytes=VMEM_LIMIT)


def _layer_spec(a, l):
    return pl.BlockSpec((None,) + a.shape[1:], lambda *_: (l,) + (0,) * (a.ndim - 1))


def _in_proj_kernel(x_ref, g_ref, w_ref, wif_ref, *rest, kv_out):
    if kv_out:
        proj_ref, pif_ref, xn_ref, k_ref, v_ref = rest[-5:]
    else:
        proj_ref, pif_ref, xn_ref = rest
    j = pl.program_id(1)

    @pl.when(j == 0)
    def _():
        x = x_ref[...]
        ms = jnp.mean(x * x, axis=-1, keepdims=True)
        xn = (x * lax.rsqrt(ms + EPS) * g_ref[...]).astype(BF16)
        xn_ref[...] = xn
        pif_ref[...] = _dot(xn, wif_ref[...])

    proj = _dot(xn_ref[...], w_ref[...])
    proj_ref[...] = proj
    if kv_out:
        tm = proj.shape[0]
        for grp, ref in ((K_GROUP, k_ref), (V_GROUP, v_ref)):
            @pl.when(j == grp)
            def _(ref=ref):
                for h in range(HEADS):
                    ref[pl.ds(h, tm, stride=HEADS), :] = proj[:, h * LANE:(h + 1) * LANE]


def _in_proj(x, g, w_in, w_if, l, tm, kv_prev=None, kv_out=False):
    rows, d = x.shape
    depth = w_in.shape[0]
    tn = d // N_BRANCH
    n = 12 * tn
    n_i, n_j = rows // tm, n // tn
    in_specs = [pl.BlockSpec((tm, d), lambda i, j: (jnp.minimum(i + (j >= n_j // 2), n_i - 1), 0)),
                _layer_spec(g, l),
                pl.BlockSpec((None, d, tn), lambda i, j: (l, 0, j)),
                _layer_spec(w_if, l)]
    out_specs = [pl.BlockSpec((tm, tn), lambda i, j: (i, j)),
                 pl.BlockSpec((tm, LANE), lambda i, j: (i, 0)),
                 pl.BlockSpec((tm, d), lambda i, j: (i, 0))]
    out_shape = [SDS((rows, n), F32), SDS((rows, LANE), F32), SDS((rows, d), BF16)]
    args = [x, g, w_in, w_if]
    aliases = {}
    if kv_out:
        kv_spec = pl.BlockSpec((None, tm * HEADS, LANE), lambda i, j: (l, i, 0))
        out_specs += [kv_spec, kv_spec]
        out_shape += [SDS((depth, rows * HEADS, LANE), F32)] * 2
        if kv_prev is not None:
            in_specs += [pl.BlockSpec(memory_space=pl.ANY)] * 2
            args += list(kv_prev)
            aliases = {4: 3, 5: 4}
    return pl.pallas_call(
        functools.partial(_in_proj_kernel, kv_out=kv_out),
        grid=(rows // tm, n // tn),
        in_specs=in_specs,
        out_specs=out_specs,
        out_shape=out_shape,
        input_output_aliases=aliases,
        compiler_params=_params("parallel", "arbitrary"),
        name="in_proj",
    )(*args)


def _gmlp_kernel(u_ref, v_ref, z_ref, lg_ref, lb_ref, ws_ref, bst_ref, y_ref, *vn_out, n_chunks):
    v = v_ref[...]
    vc = v - jnp.mean(v, axis=-1, keepdims=True)
    var = jnp.mean(vc * vc, axis=-1, keepdims=True)
    vn = vc * lax.rsqrt(var + EPS) * lg_ref[...] + lb_ref[...]
    if vn_out:
        vn_out[0][...] = vn
    vnb = vn.astype(BF16)
    row = lax.broadcasted_iota(jnp.int32, (CHUNK, CHUNK), 0)
    col = lax.broadcasted_iota(jnp.int32, (CHUNK, CHUNK), 1)
    for g in range(HEADS):
        gs = slice(g * LANE, (g + 1) * LANE)
        wm = jnp.where(row >= col, ws_ref[g], 0.0).astype(BF16)
        bcol = bst_ref[:, g:g + 1]
        for c in range(n_chunks):
            ts = slice(c * CHUNK, (c + 1) * CHUNK)
            s = _dot(wm, vnb[ts, gs]) + bcol
            y_ref[ts, gs] = (u_ref[ts, gs] * s * _silu(z_ref[ts, gs])).astype(BF16)


def _gmlp(proj3, P, l, tt, emit_vn):
    b, t, _ = proj3.shape
    w = P['ln_g'].shape[-1]
    col = lambda c: pl.BlockSpec((None, tt, w), lambda i, j: (i, j, c))
    out_specs = [pl.BlockSpec((None, tt, w), lambda i, j: (i, j, 0))]
    out_shape = [SDS((b, t, w), BF16)]
    if emit_vn:
        out_specs.append(pl.BlockSpec((None, tt, w), lambda i, j: (i, j, 0)))
        out_shape.append(SDS((b, t, w), F32))
    params = [P[k] for k in ('ln_g', 'ln_b', 'ws', 'bs_t')]
    return pl.pallas_call(
        functools.partial(_gmlp_kernel, n_chunks=tt // CHUNK),
        grid=(b, t // tt),
        in_specs=[col(0), col(1), col(2)] + [_layer_spec(a, l) for a in params],
        out_specs=out_specs,
        out_shape=out_shape,
        compiler_params=_params("parallel", "parallel"),
        name="gmlp",
    )(proj3, proj3, proj3, *params)


def _causal_conv(buf_ref, x, cw_ref, cb_ref, tt):
    h0 = CONV_PAD - (CONV_W - 1)
    buf_ref[CONV_PAD:CONV_PAD + tt, :] = x
    y = cb_ref[...] + cw_ref[0:1, :] * buf_ref[h0:h0 + tt, :]
    for j in range(1, CONV_W):
        y = y + cw_ref[j:j + 1, :] * buf_ref[h0 + j:h0 + j + tt, :]
    buf_ref[h0:CONV_PAD, :] = x[tt - (CONV_W - 1):tt, :]
    return y


def _state_spec(a, sl):
    return pl.BlockSpec((None, None) + a.shape[2:], lambda i, j: (sl, i) + (0,) * (a.ndim - 2))


def _rglru_kernel(x_ref, z_ref, hist_ref, h0_ref, cw_ref, cb_ref, wa_ref, ba_ref, wx_ref, bx_ref, lam_ref,
                  y_ref, hist_out_ref, hlast_ref, buf_ref, hcar_ref, *, tt, t_valid, first_pos_zero):
    t = pl.program_id(1)
    h0 = CONV_PAD - (CONV_W - 1)

    @pl.when(t == 0)
    def _():
        buf_ref[h0:CONV_PAD, :] = hist_ref[...]
        hcar_ref[...] = h0_ref[...]

    x = x_ref[...]
    xc = _causal_conv(buf_ref, x, cw_ref, cb_ref, tt)
    xcb = xc.astype(BF16)
    ra, rx = [], []
    for blk in range(HEADS):
        bs = slice(blk * LANE, (blk + 1) * LANE)
        ra.append(_dot(xcb[:, bs], wa_ref[blk]))
        rx.append(_dot(xcb[:, bs], wx_ref[blk]))
    r = jax.nn.sigmoid(jnp.concatenate(ra, axis=-1) + ba_ref[...])
    i = jax.nn.sigmoid(jnp.concatenate(rx, axis=-1) + bx_ref[...])
    log_a = LRU_C * r * _log_sigmoid(lam_ref[...])
    a = jnp.exp(log_a)
    mult = jnp.sqrt(-jnp.tanh(log_a) * (a * a + 1.0))
    rowi = lax.broadcasted_iota(jnp.int32, (tt, 1), 0)
    if first_pos_zero:
        mult = jnp.where(rowi + t * tt == 0, 1.0, mult)
    u = mult * (i * xc)
    d = 1
    while d < tt:
        if d % SUBLANE == 0:
            u = jnp.concatenate([u[:d], a[d:] * u[:tt - d] + u[d:]], axis=0)
            a = jnp.concatenate([a[:d], a[d:] * a[:tt - d]], axis=0)
        else:
            keep = rowi >= d
            u = a * jnp.where(keep, pltpu.roll(u, d, axis=0), 0.0) + u
            a = a * jnp.where(keep, pltpu.roll(a, d, axis=0), 1.0)
        d *= 2
    h = a * hcar_ref[...] + u
    hcar_ref[...] = h[tt - 1:tt, :]
    y_ref[...] = (h * _silu(z_ref[...])).astype(BF16)

    @pl.when(t == pl.num_programs(1) - 1)
    def _():
        hist_out_ref[...] = x[t_valid - (CONV_W - 1):t_valid, :]
        hlast_ref[...] = h[t_valid - 1:t_valid, :]


def _rglru(proj3, st, sl, P, l, tt, t_valid, first_pos_zero):
    b, t, _ = proj3.shape
    w = P['lru_cb'].shape[-1]
    assert t_valid >= CONV_W - 1 and (t_valid == tt or t == tt)
    col = lambda c: pl.BlockSpec((None, tt, w), lambda i, j: (i, j, c))
    per_b = lambda r: pl.BlockSpec((None, r, w), lambda i, j: (i, 0, 0))
    params = [P[k] for k in ('lru_cw', 'lru_cb', 'lru_wa', 'lru_ba', 'lru_wx', 'lru_bx', 'lru_lam')]
    return pl.pallas_call(
        functools.partial(_rglru_kernel, tt=tt, t_valid=t_valid, first_pos_zero=first_pos_zero),
        grid=(b, t // tt),
        in_specs=[col(3), col(4), _state_spec(st['conv_b'], sl), _state_spec(st['h_b'], sl)]
                 + [_layer_spec(a, l) for a in params],
        out_specs=[pl.BlockSpec((None, tt, w), lambda i, j: (i, j, 0)), per_b(CONV_W - 1), per_b(1)],
        out_shape=[SDS((b, t, w), BF16), SDS((b, CONV_W - 1, w), F32), SDS((b, 1, w), F32)],
        scratch_shapes=[pltpu.VMEM((CONV_PAD + tt, w), F32), pltpu.VMEM((1, w), F32)],
        compiler_params=_params("parallel", "arbitrary"),
        name="rglru",
    )(proj3, proj3, st['conv_b'], st['h_b'], *params)


def _later_matrix(n):
    row = lax.broadcasted_iota(jnp.int32, (n, n), 0)
    col = lax.broadcasted_iota(jnp.int32, (n, n), 1)
    return (row > col).astype(BF16)


def _sb_logits(z):
    log_b = jnp.minimum(z, 0.0) - jnp.log(1.0 + jnp.exp(-jnp.abs(z)))
    return log_b, log_b - z


def _sb_span(qb, kspan, vspan, r, later, bias, mask):
    cb = later.shape[0]
    log_b, log_1mb = _sb_logits(_dot_nt(qb, kspan) + bias)
    if mask is not None:
        log_1mb = jnp.where(mask, log_1mb, 0.0)
    l1b = log_1mb.astype(BF16)
    blocks = [slice(j * cb, (j + 1) * cb) for j in range(kspan.shape[0] // cb)]
    suffix = [_dot(l1b[:, js], later) for js in blocks]
    atts = []
    for js, suf in zip(reversed(blocks), reversed(suffix)):
        att = jnp.exp(log_b[:, js] + suf + r)
        if mask is not None:
            att = jnp.where(mask[:, js], att, 0.0)
        atts.insert(0, att.astype(BF16))
        r = r + jnp.sum(log_1mb[:, js], axis=-1, keepdims=True)
    return _dot(jnp.concatenate(atts, axis=-1), vspan), r


def _sb_attn_kernel(bias_ref, q_ref, k_ref, v_ref, z_ref, y_ref, kb_ref, vb_ref, acc_ref, r_ref, *,
                    layer, tq, cb, scale):
    qi = pl.program_id(2)

    @pl.when(qi == 0)
    def _():
        kb_ref[...] = k_ref[...].astype(BF16)
        vb_ref[...] = v_ref[...].astype(BF16)

    bias = bias_ref[layer, pl.program_id(1)]
    later = _later_matrix(cb)
    qb = (q_ref[...] * scale).astype(BF16)

    def span(k0, n, r, mask):
        return _sb_span(qb, kb_ref[pl.ds(k0, n), :], vb_ref[pl.ds(k0, n), :], r, later, bias, mask)

    rr = lax.broadcasted_iota(jnp.int32, (tq, tq), 0)
    cc = lax.broadcasted_iota(jnp.int32, (tq, tq), 1)
    acc_ref[...], r_ref[...] = span(pl.multiple_of(qi * tq, tq), tq, jnp.zeros((tq, 1), F32), cc < rr)

    def body(i, carry):
        pv, r_ref[...] = span(pl.multiple_of((qi - 2 - 2 * i) * tq, tq), 2 * tq, r_ref[...], None)
        acc_ref[...] += pv
        return carry

    lax.fori_loop(0, qi // 2, body, 0)

    @pl.when(qi % 2 == 1)
    def _():
        pv, r_ref[...] = span(0, tq, r_ref[...], None)
        acc_ref[...] += pv

    y_ref[...] = (acc_ref[...] * _silu(z_ref[...])).astype(BF16)


def _sb_attn(proj3, bias, l, tq):
    b, t, _ = proj3.shape
    hd = LANE
    cb = min(tq, 2 * CHUNK)
    assert tq % cb == 0
    qcol, kcol, vcol, zcol = (5 * HEADS, K_GROUP * HEADS, V_GROUP * HEADS, 8 * HEADS)
    tile = lambda c0: pl.BlockSpec((None, tq, hd), lambda i, h, j: (i, j, c0 + h))
    whole = lambda c0: pl.BlockSpec((None, t, hd), lambda i, h, j: (i, 0, c0 + h))
    return pl.pallas_call(
        functools.partial(_sb_attn_kernel, layer=l, tq=tq, cb=cb, scale=hd ** -0.5),
        grid=(b, HEADS, t // tq),
        in_specs=[pl.BlockSpec(memory_space=pltpu.SMEM), tile(qcol), whole(kcol), whole(vcol), tile(zcol)],
        out_specs=pl.BlockSpec((None, tq, hd), lambda i, h, j: (i, j, h)),
        out_shape=SDS((b, t, HEADS * hd), BF16),
        scratch_shapes=[pltpu.VMEM((t, hd), BF16), pltpu.VMEM((t, hd), BF16),
                        pltpu.VMEM((tq, hd), F32), pltpu.VMEM((tq, 1), F32)],
        compiler_params=_params("parallel", "parallel", "arbitrary"),
        name="sb_attn",
    )(bias, proj3, proj3, proj3, proj3)


def _sb_decode_kernel(pt_ref, bias_ref, q_ref, kn_ref, vn_ref, z_ref, *rest, layer, n_pages, scale):
    k_pages, v_pages = rest[:n_pages], rest[n_pages:2 * n_pages]
    y_ref, acc_ref, r_ref = rest[2 * n_pages:]
    s = pl.program_id(1)
    later = _later_matrix(CHUNK)

    def update(keys, values, n_blk, masked):
        zs = []
        for h in range(HEADS):
            kh = jnp.concatenate([keys(h, g) for g in range(n_blk)], axis=0).astype(BF16)
            zh = _dot_nt(q_ref[:, h * LANE:(h + 1) * LANE].astype(BF16), kh) * scale + bias_ref[layer, h]
            zs += [zh[:, g * CHUNK:(g + 1) * CHUNK] for g in range(n_blk)]
        log_b, log_1mb = _sb_logits(jnp.concatenate(zs, axis=0))
        if masked:
            rr = lax.broadcasted_iota(jnp.int32, log_b.shape, 0) & (DEC_ROWS - 1)
            mask = lax.broadcasted_iota(jnp.int32, log_b.shape, 1) < rr
            log_1mb = jnp.where(mask, log_1mb, 0.0)
        suffix = _dot(log_1mb.astype(BF16), later)
        total = jnp.sum(log_1mb, axis=-1, keepdims=True)
        for h in range(HEADS):
            run = r_ref[h]
            atts = []
            for g in range(n_blk):
                rows = slice((h * n_blk + g) * DEC_ROWS, (h * n_blk + g + 1) * DEC_ROWS)
                att = jnp.exp(log_b[rows] + suffix[rows] + run)
                if masked:
                    att = jnp.where(mask[rows], att, 0.0)
                atts.append(att.astype(BF16))
                run = run + total[rows]
            vh = jnp.concatenate([values(h, g) for g in range(n_blk)], axis=0).astype(BF16)
            acc_ref[h] += _dot(jnp.concatenate(atts, axis=-1), vh)
            r_ref[h] = run

    @pl.when(s == 0)
    def _():
        acc_ref[...] = jnp.zeros_like(acc_ref)
        r_ref[...] = jnp.zeros_like(r_ref)
        update(lambda h, g: kn_ref[:, h * LANE:(h + 1) * LANE], lambda h, g: vn_ref[:, h * LANE:(h + 1) * LANE],
               1, True)

    update(lambda h, g: k_pages[g][pl.ds(h, CHUNK, stride=HEADS), :],
           lambda h, g: v_pages[g][pl.ds(h, CHUNK, stride=HEADS), :], n_pages, False)

    @pl.when(s == pl.num_programs(1) - 1)
    def _():
        acc = jnp.concatenate([acc_ref[h] for h in range(HEADS)], axis=-1)
        y_ref[...] = (acc * _silu(z_ref[...])).astype(BF16)


def _sb_decode(page_table, bias, q, k_new, v_new, z, cache_k, cache_v, layer, pages_per_step):
    b, n_tab = page_table.shape
    w = q.shape[-1]
    g = pages_per_step
    assert n_tab % g == 0
    page_rows, hd = cache_k.shape[2:]
    per_b = lambda r: pl.BlockSpec((None, r, w), lambda i, s, pt, bs: (i, 0, 0))

    def page_spec(j):
        return pl.BlockSpec((None, None, page_rows, hd),
                            lambda i, s, pt, bs: (layer, pt[i, n_tab - 1 - (s * g + j)], 0, 0))

    grid_spec = pltpu.PrefetchScalarGridSpec(
        num_scalar_prefetch=2,
        grid=(b, n_tab // g),
        in_specs=[per_b(DEC_ROWS), per_b(CHUNK), per_b(CHUNK), per_b(DEC_ROWS)]
                 + [page_spec(j) for j in range(g)] * 2,
        out_specs=per_b(DEC_ROWS),
        scratch_shapes=[pltpu.VMEM((HEADS, DEC_ROWS, hd), F32), pltpu.VMEM((HEADS, DEC_ROWS, hd), F32)],
    )
    return pl.pallas_call(
        functools.partial(_sb_decode_kernel, layer=layer, n_pages=g, scale=hd ** -0.5),
        grid_spec=grid_spec,
        out_shape=SDS((b, DEC_ROWS, w), BF16),
        compiler_params=_params("parallel", "arbitrary"),
        name="sb_decode",
    )(page_table, bias, q, k_new, v_new, z, *([cache_k] * g), *([cache_v] * g))


def _mlstm_kernel(x_ref, z_ref, o_ref, gc_ref, gr_ref, hist_ref, c0_ref, n0_ref, m0_ref,
                  cw_ref, cb_ref, wq_ref, wk_ref, wv_ref, brow_ref, bcol_ref, ng_ref, sk_ref,
                  y_ref, hist_out_ref, c_out_ref, n_out_ref, m_out_ref,
                  buf_ref, c_scr, n_scr, m_scr, *, t_valid, scale, n_chunks):
    step = pl.program_id(1)
    L = CHUNK
    rows = n_chunks * L
    h0 = CONV_PAD - (CONV_W - 1)

    @pl.when(step == 0)
    def _():
        buf_ref[h0:CONV_PAD, :] = hist_ref[...]
        c_scr[...] = c0_ref[...]
        n_scr[...] = n0_ref[...]
        m_scr[...] = m0_ref[...]

    x = x_ref[...]
    xc = _silu(_causal_conv(buf_ref, x, cw_ref, cb_ref, rows))
    xcb = xc.astype(BF16)
    xb = x.astype(BF16)
    gc = gc_ref[...] + brow_ref[...]
    gr = gr_ref[...] + bcol_ref[...]
    row = lax.broadcasted_iota(jnp.int32, (L, L), 0)
    col = lax.broadcasted_iota(jnp.int32, (L, L), 1)
    causal = row >= col
    tril, triu = causal.astype(BF16), (row <= col).astype(BF16)
    lsc, lsr = _log_sigmoid(gc), _log_sigmoid(gr)
    rowi = lax.broadcasted_iota(jnp.int32, (L, 1), 0)
    H = range(HEADS)
    CH = [(ci, h) for ci in range(n_chunks) for h in H]
    rs = [slice(ci * L, (ci + 1) * L) for ci in range(n_chunks)]
    hs = [slice(h * LANE, (h + 1) * LANE) for h in H]
    n_all, m_all = n_scr[...], m_scr[...]
    bc_all = [_cumsum_dot_left(tril, lsc[rs[ci], :]) for ci in range(n_chunks)]
    br_all = [_cumsum_dot_right(lsr[:, rs[ci]], triu) for ci in range(n_chunks)]
    q = {p: _dot(xcb[rs[p[0]], hs[p[1]]], wq_ref[p[1]]) for p in CH}
    k = {p: _dot(xcb[rs[p[0]], hs[p[1]]], wk_ref[p[1]]) * scale for p in CH}
    vb = {p: _dot(xb[rs[p[0]], hs[p[1]]], wv_ref[p[1]]).astype(BF16) for p in CH}
    qb = {p: q[p].astype(BF16) for p in CH}
    qk = {p: _dot_nt(qb[p], k[p].astype(BF16)) for p in CH}
    ig_c = {(ci, h): gc[rs[ci], h:h + 1] for ci, h in CH}
    b_c = {(ci, h): bc_all[ci][:, HEADS + h:HEADS + h + 1] for ci, h in CH}
    dmat = {(ci, h): jnp.where(causal, b_c[ci, h] - br_all[ci][HEADS + h:HEADS + h + 1, :] + gr[h:h + 1, rs[ci]],
                               -jnp.inf) for ci, h in CH}
    dmax = {p: jnp.max(dmat[p], axis=-1, keepdims=True) for p in CH}
    m_h, m_new, decay, kw = {}, {}, {}, {}
    for ci, h in CH:
        m_h[ci, h] = m_all[h:h + 1, 0:1] if ci == 0 else m_new[ci - 1, h]
        b_last = b_c[ci, h][t_valid - 1:t_valid, :]
        g = b_last - b_c[ci, h] + ig_c[ci, h]
        if t_valid < L:
            g = jnp.where(rowi < t_valid, g, -jnp.inf)
        m_new[ci, h] = jnp.maximum(b_last + m_h[ci, h], jnp.max(g, axis=0, keepdims=True))
        kw[ci, h] = k[ci, h] * jnp.exp(g - m_new[ci, h])
        decay[ci, h] = jnp.exp(b_last + m_h[ci, h] - m_new[ci, h])
    inter = {p: b_c[p] + m_h[p] for p in CH}
    m_t = {p: jnp.maximum(dmax[p], inter[p]) for p in CH}
    s = {p: qk[p] * jnp.exp(dmat[p] - m_t[p]) for p in CH}
    w_inter = {p: jnp.exp(inter[p] - m_t[p]) for p in CH}
    sv = {p: _dot(s[p].astype(BF16), vb[p]) for p in CH}
    kv = {p: _dot(kw[p].T.astype(BF16), vb[p]) for p in CH}
    qc, qn = {}, {}
    c_cur = [c_scr[h] for h in H]
    n_cur = [n_all[h:h + 1, :] for h in H]
    for ci, h in CH:
        qc[ci, h] = _dot(qb[ci, h], c_cur[h].astype(BF16))
        qn[ci, h] = jnp.sum(q[ci, h] * n_cur[h], axis=-1, keepdims=True)
        c_cur[h] = decay[ci, h] * c_cur[h] + kv[ci, h]
        n_cur[h] = decay[ci, h] * n_cur[h] + jnp.sum(kw[ci, h], axis=0, keepdims=True)
    for h in H:
        c_scr[h] = c_cur[h]
    n_scr[...] = jnp.concatenate(n_cur, axis=0)
    m_scr[...] = jnp.concatenate([jnp.broadcast_to(m_new[n_chunks - 1, h], (1, LANE)) for h in H], axis=0)
    num = {p: sv[p] + w_inter[p] * qc[p] for p in CH}
    den = {p: jnp.sum(s[p], axis=-1, keepdims=True) + w_inter[p] * qn[p] for p in CH}
    hh = {p: num[p] / jnp.maximum(jnp.abs(den[p]), jnp.exp(-m_t[p])) for p in CH}
    hh = {(ci, h): jax.nn.sigmoid(o_ref[rs[ci], hs[h]]) * hh[ci, h] for ci, h in CH}
    hc = {p: hh[p] - jnp.mean(hh[p], axis=-1, keepdims=True) for p in CH}
    outs = {p: hc[p] * lax.rsqrt(jnp.mean(hc[p] * hc[p], axis=-1, keepdims=True) + EPS) for p in CH}
    hn = jnp.concatenate([jnp.concatenate([outs[ci, h] for h in H], axis=-1) for ci in range(n_chunks)], axis=0)
    hn = hn * ng_ref[...] + sk_ref[...] * xc
    y_ref[...] = (hn * _silu(z_ref[...])).astype(BF16)

    @pl.when(step == pl.num_programs(1) - 1)
    def _():
        last = (n_chunks - 1) * L + t_valid
        hist_out_ref[...] = x[last - (CONV_W - 1):last, :]
        c_out_ref[...] = c_scr[...]
        n_out_ref[...] = n_scr[...]
        m_out_ref[...] = m_scr[...]


def _mlstm(proj3, gates_col, gates_row, st, sl, P, l, t_valid):
    b, t, _ = proj3.shape
    w = P['ml_cb'].shape[-1]
    hd = w // HEADS
    L = CHUNK
    assert t_valid >= CONV_W - 1 and (t_valid == L or t == L)
    n_chunks = 8 if t % (8 * L) == 0 else 1
    rows = n_chunks * L
    col = lambda c: pl.BlockSpec((None, rows, w), lambda i, j: (i, j, c))
    per_b = lambda *s: pl.BlockSpec((None,) + s, lambda i, j: (i,) + (0,) * len(s))
    states = [st[k] for k in ('conv_d', 'C', 'n', 'm')]
    params = [P[k] for k in ('ml_cw', 'ml_cb', 'ml_wq', 'ml_wk', 'ml_wv', 'ml_brow', 'ml_bcol', 'ml_ng', 'ml_sk')]
    return pl.pallas_call(
        functools.partial(_mlstm_kernel, t_valid=t_valid, scale=hd ** -0.5, n_chunks=n_chunks),
        grid=(b, t // rows),
        in_specs=[col(9), col(10), col(11),
                  pl.BlockSpec((None, rows, LANE), lambda i, j: (i, j, 0)),
                  pl.BlockSpec((None, 2 * HEADS, rows), lambda i, j: (i, 0, j))]
                 + [_state_spec(a, sl) for a in states] + [_layer_spec(a, l) for a in params],
        out_specs=[pl.BlockSpec((None, rows, w), lambda i, j: (i, j, 0)),
                   per_b(CONV_W - 1, w), per_b(HEADS, hd, hd), per_b(HEADS, hd), per_b(HEADS, LANE)],
        out_shape=[SDS((b, t, w), BF16), SDS((b, CONV_W - 1, w), F32), SDS((b, HEADS, hd, hd), F32),
                   SDS((b, HEADS, hd), F32), SDS((b, HEADS, LANE), F32)],
        scratch_shapes=[pltpu.VMEM((CONV_PAD + rows, w), F32), pltpu.VMEM((HEADS, hd, hd), F32),
                        pltpu.VMEM((HEADS, hd), F32), pltpu.VMEM((HEADS, LANE), F32)],
        compiler_params=_params("parallel", "arbitrary"),
        name="mlstm",
    )(proj3, proj3, proj3, gates_col, gates_row, *states, *params)


def _merge_kernel(xn_ref, a_ref, b_ref, c_ref, d_ref, wg0_ref, wg1_ref, wg2_ref, wg3_ref, bg_ref, wb_ref, o_ref):
    xn = xn_ref[...]
    acc = None
    branches = (a_ref, b_ref, c_ref, d_ref)
    gates = (wg0_ref, wg1_ref, wg2_ref, wg3_ref)
    for m in range(N_BRANCH):
        gate = jax.nn.sigmoid(_dot(xn, gates[m][...]) + bg_ref[m:m + 1, :])
        term = gate * _dot(branches[m][...], wb_ref[m])
        acc = term if acc is None else acc + term
    o_ref[...] = acc.astype(BF16)


def _merge(xn, branches, P, l, tm, tn):
    rows, d = xn.shape
    w = branches[0].shape[-1]
    nj = d // tn
    gate_spec = lambda m: pl.BlockSpec((None, d, tn), lambda i, j: (l, 0, m * nj + j))
    return pl.pallas_call(
        _merge_kernel,
        grid=(rows // tm, nj),
        in_specs=[pl.BlockSpec((tm, d), lambda i, j: (i, 0))]
                 + [pl.BlockSpec((tm, w), lambda i, j: (i, 0))] * N_BRANCH
                 + [gate_spec(m) for m in range(N_BRANCH)]
                 + [pl.BlockSpec((None, N_BRANCH, tn), lambda i, j: (l, 0, j)),
                    pl.BlockSpec((None, N_BRANCH, w, tn), lambda i, j: (l, 0, 0, j))],
        out_specs=pl.BlockSpec((tm, tn), lambda i, j: (i, j)),
        out_shape=SDS((rows, d), BF16),
        compiler_params=_params("parallel", "arbitrary"),
        name="merge",
    )(xn, *branches, *([P['w_gate']] * N_BRANCH), P['b_gate'], P['w_branch'])


def _out_proj_kernel(m_ref, x_ref, w_ref, g_ref, o_ref):
    out = _dot(m_ref[...], w_ref[...])
    ms = jnp.mean(out * out, axis=-1, keepdims=True)
    o_ref[...] = x_ref[...] + out * lax.rsqrt(ms + EPS) * g_ref[...]


def _out_proj(merged, x, P, l, tm):
    rows, d = x.shape
    return pl.pallas_call(
        _out_proj_kernel,
        grid=(rows // tm,),
        in_specs=[pl.BlockSpec((tm, d), lambda i: (i, 0)),
                  pl.BlockSpec((tm, d), lambda i: (i, 0)),
                  _layer_spec(P['w_out'], l),
                  _layer_spec(P['norm_post'], l)],
        out_specs=pl.BlockSpec((tm, d), lambda i: (i, 0)),
        out_shape=SDS((rows, d), F32),
        compiler_params=_params("parallel"),
        name="out_proj",
    )(merged, x, P['w_out'], P['norm_post'])


def _unwritten(shape, dtype, name):
    return pl.pallas_call(lambda o_ref: None, out_shape=SDS(shape, dtype),
                          out_specs=pl.BlockSpec(memory_space=pl.ANY), name=name)()


def _tile(n, pref):
    return pref if n % pref == 0 else n


def _branches(proj3, gates_col, P, l, st, sl, t_valid, first_pos_zero, emit_vn):
    b, t, _ = proj3.shape
    gm = _gmlp(proj3, P, l, _tile(t, 8 * CHUNK), emit_vn)
    tt = _tile(t, 4 * CHUNK) if t_valid == CHUNK else t
    y_b, hist_b, h_b = _rglru(proj3, st, sl, P, l, tt, tt if t_valid == CHUNK else t_valid, first_pos_zero)
    gates_row = jnp.swapaxes(gates_col[:, :, :2 * HEADS], 1, 2)
    y_d, hist_d, c, n, m = _mlstm(proj3, gates_col, gates_row, st, sl, P, l, t_valid)
    new_st = dict(conv_b=hist_b, h_b=h_b[:, 0], conv_d=hist_d, C=c, n=n, m=m[:, :, 0])
    return gm, y_b, y_d, new_st


def _dense_tail(x, xn, branches, P, l, tm):
    merged = _merge(xn, branches, P, l, tm, _tile(x.shape[1], 512))
    return _out_proj(merged, x, P, l, _tile(x.shape[0], 512))


def _state_in(conv_b, h_b, conv_d, c, n, m):
    return dict(conv_b=conv_b, h_b=h_b[:, :, None, :], conv_d=conv_d, C=c, n=n,
                m=jnp.broadcast_to(m[..., None], m.shape + (LANE,)))


def kernel(x_prompt, x_sample, cache_k, cache_v, page_table, state_rglru_conv, state_rglru_h, state_mlstm_conv, state_mlstm_c, state_mlstm_n, state_mlstm_m, norm_pre, norm_post, w_in, gmlp_ln_g, gmlp_ln_b, gmlp_ws, gmlp_bs, lru_conv_w, lru_conv_b, lru_wa, lru_ba, lru_wx, lru_bx, lru_lambda, ml_conv_w, ml_conv_b, ml_wq, ml_wk, ml_wv, ml_bi, ml_bf, ml_norm_g, ml_skip, sb_bias, w_branch, w_gate, b_gate, w_out):
    bp, tp, d = x_prompt.shape
    bs, ts, _ = x_sample.shape
    depth = w_in.shape[0]
    w = d // N_BRANCH
    hd = w // HEADS
    n_main = 12 * w
    page = cache_k.shape[2]
    past_len = page_table.shape[1] * page
    assert tp % CHUNK == 0 and page == CHUNK and ts <= DEC_ROWS and cache_k.shape[3:] == (HEADS, hd)
    ck = cache_k.reshape(depth, cache_k.shape[1], page * HEADS, hd)
    cv = cache_v.reshape(depth, cache_v.shape[1], page * HEADS, hd)
    n_tab = page_table.shape[1]
    pages_per_step = next(g for g in (16, 8, 4, 2, 1) if n_tab % g == 0)

    row3 = lambda a: a[:, None, :]
    gate_bias = jnp.concatenate([ml_bi, ml_bf], axis=-1)
    P = dict(
        norm_pre=row3(norm_pre), norm_post=row3(norm_post),
        w_in=w_in.astype(BF16),
        w_if=jnp.pad(w_in[:, :, n_main:], ((0, 0), (0, 0), (0, LANE - 2 * HEADS))).astype(BF16),
        ln_g=row3(gmlp_ln_g), ln_b=row3(gmlp_ln_b), ws=gmlp_ws, bs_t=jnp.swapaxes(gmlp_bs, 1, 2),
        lru_cw=lru_conv_w, lru_cb=row3(lru_conv_b), lru_wa=lru_wa.astype(BF16), lru_ba=row3(lru_ba),
        lru_wx=lru_wx.astype(BF16), lru_bx=row3(lru_bx), lru_lam=row3(lru_lambda),
        ml_cw=ml_conv_w, ml_cb=row3(ml_conv_b), ml_wq=ml_wq.astype(BF16), ml_wk=ml_wk.astype(BF16),
        ml_wv=ml_wv.astype(BF16), ml_brow=row3(jnp.pad(gate_bias, ((0, 0), (0, LANE - 2 * HEADS)))),
        ml_bcol=gate_bias[:, :, None], ml_ng=row3(ml_norm_g), ml_sk=row3(ml_skip),
        w_gate=w_gate.astype(BF16), b_gate=b_gate.reshape(depth, N_BRANCH, d),
        w_branch=w_branch.astype(BF16), w_out=w_out.astype(BF16))

    xp = x_prompt.reshape(bp * tp, d)
    xs = x_sample.reshape(bs * ts, d)
    st_p = _state_in(jnp.zeros((1, bp, CONV_W - 1, w), F32), jnp.zeros((1, bp, w), F32),
                     jnp.zeros((1, bp, CONV_W - 1, w), F32), jnp.zeros((1, bp, HEADS, hd, hd), F32),
                     jnp.zeros((1, bp, HEADS, hd), F32), jnp.zeros((1, bp, HEADS), F32))
    st_s = _state_in(state_rglru_conv, state_rglru_h, state_mlstm_conv, state_mlstm_c, state_mlstm_n, state_mlstm_m)
    names = ('conv_b', 'h_b', 'conv_d', 'C', 'n', 'm')
    res_p = {k: [] for k in names}
    res_s = {k: [] for k in names}
    ks_l, vs_l, gv_l = [], [], []
    kv_p = [_unwritten((depth, bp * tp * HEADS, LANE), F32, name) for name in ("k_stack", "v_stack")]
    pad_t = lambda a, rows: jnp.pad(a, ((0, 0), (0, rows - ts), (0, 0)))

    for l in range(depth):
        proj, pif, xn, *kv_p = _in_proj(xp, P['norm_pre'], P['w_in'], P['w_if'], l, _tile(bp * tp, 1024),
                                        kv_prev=kv_p, kv_out=True)
        proj3 = proj.reshape(bp, tp, n_main)
        y_a, y_b, y_d, nst = _branches(proj3, pif.reshape(bp, tp, LANE), P, l, st_p, 0, CHUNK, True, False)
        y_c = _sb_attn(proj3, sb_bias, l, _tile(tp, 4 * CHUNK))
        xp = _dense_tail(xp, xn, [y.reshape(bp * tp, w) for y in (y_a[0], y_b, y_c, y_d)], P, l,
                         _tile(bp * tp, 1024))
        for k in names:
            res_p[k].append(nst[k])

        proj, pif, xn = _in_proj(xs, P['norm_pre'], P['w_in'], P['w_if'], l, bs * ts)
        proj3 = proj.reshape(bs, ts, n_main)
        projc = pad_t(proj3, CHUNK)
        (y_a, vn), y_b, y_d, nst = _branches(projc, pad_t(pif.reshape(bs, ts, LANE), CHUNK), P, l, st_s, l, ts,
                                             past_len == 0, True)
        y_c = _sb_decode(page_table, sb_bias, pad_t(proj3[:, :, 5 * w:6 * w], DEC_ROWS),
                         projc[:, :, K_GROUP * w:(K_GROUP + 1) * w], projc[:, :, V_GROUP * w:(V_GROUP + 1) * w],
                         pad_t(proj3[:, :, 8 * w:9 * w], DEC_ROWS), ck, cv, l, pages_per_step)
        xs = _dense_tail(xs, xn, [y[:, :ts].reshape(bs * ts, w) for y in (y_a, y_b, y_c, y_d)], P, l, bs * ts)
        for k in names:
            res_s[k].append(nst[k])
        ks_l.append(proj3[:, :, K_GROUP * w:(K_GROUP + 1) * w].reshape(bs, ts, HEADS, hd))
        vs_l.append(proj3[:, :, V_GROUP * w:(V_GROUP + 1) * w].reshape(bs, ts, HEADS, hd))
        gv_l.append(vn[:, :ts])

    st = lambda lst: jnp.stack(lst, axis=0)
    k_p, v_p = (a.reshape(depth, bp, tp, HEADS, hd) for a in kv_p)
    return (xp.reshape(bp, tp, d), xs.reshape(bs, ts, d), k_p, v_p, st(ks_l), st(vs_l),
            st(res_p['conv_b']), st(res_p['h_b']), st(res_s['conv_b']), st(res_s['h_b']),
            st(res_p['conv_d']), st(res_p['C']), st(res_p['n']), st(res_p['m']),
            st(res_s['conv_d']), st(res_s['C']), st(res_s['n']), st(res_s['m']),
            st(gv_l))
```
